```python
import math
import jax, jax.numpy as jnp
from jax import lax
import numpy as np

D_MODEL = 2048
BATCH = 8
SEQ = 4096
DEPTH = 4

CHUNK = 64
N_A_LAYERS = DEPTH // 2
N_B_LAYERS = DEPTH - N_A_LAYERS
N_DENSE_LAYERS = (DEPTH + 1) // 2
N_MOE_LAYERS = DEPTH // 2
SSM_WIDTH = D_MODEL
SSM_GROUP = 16
SSM_GROUPS = SSM_WIDTH // SSM_GROUP
SSM_STATE = 64
ATTN_HEADS = 32
HEAD_DIM = D_MODEL // ATTN_HEADS
LEFT_CHUNKS = 8
BAND_CHUNKS = LEFT_CHUNKS + 1
BAND = BAND_CHUNKS * CHUNK
REL_CLIP = 128
REL_TABLE = REL_CLIP + CHUNK
D_FF = 5504
N_EXPERTS = 8
TOP_K = 2
D_FF_EXPERT = 7168
MOE_BLOCK = 256
DEEPNORM_ALPHA = (2.0 * DEPTH) ** 0.25
DEEPNORM_BETA = (8.0 * DEPTH) ** -0.25
LN_EPS = 1e-5

kernel_name = "yoco_s5_chunkattn_moe_deepnorm"


def layer_norm(x, g, b):
    xf = x.astype(jnp.float32)
    mu = jnp.mean(xf, axis=-1, keepdims=True)
    var = jnp.mean(jnp.square(xf - mu), axis=-1, keepdims=True)
    y = (xf - mu) * lax.rsqrt(var + LN_EPS)
    return (y * g.astype(jnp.float32) + b.astype(jnp.float32)).astype(x.dtype)


def _complex_combine(e1, e2):
    a1r, a1i, b1r, b1i = e1
    a2r, a2i, b2r, b2i = e2
    ar = a2r * a1r - a2i * a1i
    ai = a2r * a1i + a2i * a1r
    br = a2r * b1r - a2i * b1i + b2r
    bi = a2r * b1i + a2i * b1r + b2i
    return (ar, ai, br, bi)


def s5_mixer(x, w_in, log_step, lam_re, lam_im, b_re, b_im, c_re, c_im, d_skip, w_glu):
    bsz, seq, _ = x.shape
    nc = seq // CHUNK
    f32 = jnp.float32
    u = (x @ w_in).astype(f32)
    lr, li = lam_re.astype(f32), lam_im.astype(f32)
    delta = jnp.exp(log_step.astype(f32))[:, None]
    mag = jnp.exp(lr * delta)
    ebar_r = mag * jnp.cos(li * delta)
    ebar_i = mag * jnp.sin(li * delta)
    denom = lr * lr + li * li
    num_r = ebar_r - 1.0
    num_i = ebar_i
    coef_r = (num_r * lr + num_i * li) / denom
    coef_i = (num_i * lr - num_r * li) / denom
    br, bi = b_re.astype(f32), b_im.astype(f32)
    bbar_r = coef_r[..., None] * br - coef_i[..., None] * bi
    bbar_i = coef_r[..., None] * bi + coef_i[..., None] * br
    cr, ci = c_re.astype(f32), c_im.astype(f32)
    a_r = jnp.broadcast_to(ebar_r, (bsz, CHUNK, SSM_GROUPS, SSM_STATE))
    a_i = jnp.broadcast_to(ebar_i, (bsz, CHUNK, SSM_GROUPS, SSM_STATE))
    u_chunks = u.reshape(bsz, nc, CHUNK, SSM_GROUPS, SSM_GROUP).transpose(1, 0, 2, 3, 4)

    def step(carry, u_c):
        hr, hi = carry
        bur = jnp.einsum("bcgm,gpm->bcgp", u_c, bbar_r)
        bui = jnp.einsum("bcgm,gpm->bcgp", u_c, bbar_i)
        car, cai, cbr, cbi = lax.associative_scan(_complex_combine, (a_r, a_i, bur, bui), axis=1)
        sr = car * hr[:, None] - cai * hi[:, None] + cbr
        si = car * hi[:, None] + cai * hr[:, None] + cbi
        y = jnp.einsum("gmp,bcgp->bcgm", cr, sr) - jnp.einsum("gmp,bcgp->bcgm", ci, si)
        return (sr[:, -1], si[:, -1]), y

    h0 = jnp.zeros((bsz, SSM_GROUPS, SSM_STATE), f32)
    _, ys = lax.scan(step, (h0, h0), u_chunks)
    y = ys.transpose(1, 0, 2, 3, 4).reshape(bsz, seq, SSM_WIDTH) + d_skip.astype(f32) * u
    g = jax.nn.gelu(y).astype(x.dtype)
    val, gate = jnp.split(g @ w_glu, 2, axis=-1)
    return val * jax.nn.sigmoid(gate)


def shared_kv(h, w_k, w_v):
    bsz, seq, _ = h.shape
    nc = seq // CHUNK
    pad = ((0, 0), (LEFT_CHUNKS, 0), (0, 0), (0, 0), (0, 0))
    k = (h @ w_k).reshape(bsz, nc, CHUNK, ATTN_HEADS, HEAD_DIM)
    v = (h @ w_v).reshape(bsz, nc, CHUNK, ATTN_HEADS, HEAD_DIM)
    return jnp.pad(k, pad), jnp.pad(v, pad)


def chunk_attention(x, k_pad, v_pad, w_q, rel_bias, w_o):
    bsz, seq, _ = x.shape
    nc = seq // CHUNK
    q = (x @ w_q).reshape(bsz, nc, CHUNK, ATTN_HEADS, HEAD_DIM).transpose(1, 0, 2, 3, 4)
    qi = jnp.arange(CHUNK)[:, None]
    kj = jnp.arange(BAND)[None, :]
    dist = qi + LEFT_CHUNKS * CHUNK - kj
    idx = jnp.clip(dist, -(CHUNK - 1), REL_CLIP) + (CHUNK - 1)
    bias = rel_bias[:, idx].astype(jnp.float32)
    key_chunk = jnp.repeat(jnp.arange(BAND_CHUNKS), CHUNK)
    scale = HEAD_DIM ** -0.5

    def one_chunk(args):
        c, q_c = args
        k_b = lax.dynamic_slice_in_dim(k_pad, c, BAND_CHUNKS, axis=1).reshape(bsz, BAND, ATTN_HEADS, HEAD_DIM)
        v_b = lax.dynamic_slice_in_dim(v_pad, c, BAND_CHUNKS, axis=1).reshape(bsz, BAND, ATTN_HEADS, HEAD_DIM)
        s = jnp.einsum("bqhd,bkhd->bhqk", q_c, k_b).astype(jnp.float32) * scale + bias
        valid = (key_chunk + c) >= LEFT_CHUNKS
        s = jnp.where(valid[None, None, None, :], s, -1e30)
        p = jax.nn.softmax(s, axis=-1).astype(v_b.dtype)
        return jnp.einsum("bhqk,bkhd->bqhd", p, v_b)

    o = lax.map(one_chunk, (jnp.arange(nc), q))
    o = o.transpose(1, 0, 2, 3, 4).reshape(bsz, seq, ATTN_HEADS * HEAD_DIM)
    return o @ w_o


def swiglu_ffn(x, w1, w3, w2):
    return (jax.nn.silu(x @ w1) * (x @ w3)) @ w2


def moe_ffn(x, w_router, w1, w3, w2):
    bsz, seq, d = x.shape
    n_tok = bsz * seq
    xf = x.reshape(n_tok, d)
    logits = (xf @ w_router).astype(jnp.float32)
    top_vals, top_idx = lax.top_k(logits, TOP_K)
    gates = jax.nn.softmax(top_vals, axis=-1)
    n_assign = n_tok * TOP_K
    e_flat = top_idx.reshape(-1).astype(jnp.int32)
    g_flat = gates.reshape(-1)
    order = jnp.argsort(e_flat)
    e_s = e_flat[order]
    tok_s = (order // TOP_K).astype(jnp.int32)
    g_s = g_flat[order]
    counts = jnp.zeros((N_EXPERTS,), jnp.int32).at[e_flat].add(1)
    padded = ((counts + MOE_BLOCK - 1) // MOE_BLOCK) * MOE_BLOCK
    start = jnp.cumsum(counts) - counts
    pend = jnp.cumsum(padded)
    pstart = pend - padded
    dest = pstart[e_s] + (jnp.arange(n_assign, dtype=jnp.int32) - start[e_s])
    p_max = ((n_assign + MOE_BLOCK - 1) // MOE_BLOCK) * MOE_BLOCK + N_EXPERTS * MOE_BLOCK
    n_blocks = p_max // MOE_BLOCK
    buf_tok = jnp.full((p_max,), n_tok, jnp.int32).at[dest].set(tok_s)
    buf_gate = jnp.zeros((p_max,), jnp.float32).at[dest].set(g_s)
    block_exp = jnp.minimum(
        jnp.searchsorted(pend, jnp.arange(n_blocks, dtype=jnp.int32) * MOE_BLOCK, side="right"),
        N_EXPERTS - 1).astype(jnp.int32)
    x_pad = jnp.concatenate([xf, jnp.zeros((1, d), xf.dtype)], axis=0)
    xb = x_pad[buf_tok].reshape(n_blocks, MOE_BLOCK, d)

    def expert_block(args):
        xblk, e = args
        return (jax.nn.silu(xblk @ w1[e]) * (xblk @ w3[e])) @ w2[e]

    yb = lax.map(expert_block, (xb, block_exp)).reshape(p_max, d)
    yb = yb * buf_gate[:, None].astype(yb.dtype)
    out = jnp.zeros((n_tok + 1, d), yb.dtype).at[buf_tok].add(yb)[:n_tok]
    return out.reshape(bsz, seq, d)


def setup_inputs(seed: int = 0) -> dict:
    key = jax.random.key(seed)
    ks = jax.random.split(key, 26)
    f32 = jnp.float32

    def nrm(k, shape, std):
        return jax.random.normal(k, shape, f32) * std

    na, nb, nd, nm = N_A_LAYERS, N_B_LAYERS, N_DENSE_LAYERS, N_MOE_LAYERS
    beta = DEEPNORM_BETA
    hd = ATTN_HEADS * HEAD_DIM
    n_idx = jnp.arange(SSM_STATE, dtype=f32)
    return {
        "x": nrm(ks[0], (BATCH, SEQ, D_MODEL), 1.0),
        "ssm_w_in": nrm(ks[1], (na, D_MODEL, SSM_WIDTH), D_MODEL ** -0.5),
        "ssm_log_step": jax.random.uniform(ks[2], (na, SSM_GROUPS), f32, math.log(1e-3), math.log(1e-1)),
        "ssm_lam_re": -0.5 + nrm(ks[3], (na, SSM_GROUPS, SSM_STATE), 0.01),
        "ssm_lam_im": math.pi * n_idx + nrm(ks[4], (na, SSM_GROUPS, SSM_STATE), 0.01),
        "ssm_b_re": nrm(ks[5], (na, SSM_GROUPS, SSM_STATE, SSM_GROUP), (2.0 * SSM_GROUP) ** -0.5),
        "ssm_b_im": nrm(ks[6], (na, SSM_GROUPS, SSM_STATE, SSM_GROUP), (2.0 * SSM_GROUP) ** -0.5),
        "ssm_c_re": nrm(ks[7], (na, SSM_GROUPS, SSM_GROUP, SSM_STATE), 0.5),
        "ssm_c_im": nrm(ks[8], (na, SSM_GROUPS, SSM_GROUP, SSM_STATE), 0.5),
        "ssm_d": nrm(ks[9], (na, SSM_WIDTH), 0.5),
        "ssm_w_glu": nrm(ks[10], (na, SSM_WIDTH, 2 * D_MODEL), beta * SSM_WIDTH ** -0.5),
        "attn_w_k": nrm(ks[11], (D_MODEL, hd), D_MODEL ** -0.5),
        "attn_w_v": nrm(ks[12], (D_MODEL, hd), beta * D_MODEL ** -0.5),
        "attn_w_q": nrm(ks[13], (nb, D_MODEL, hd), D_MODEL ** -0.5),
        "attn_rel_bias": nrm(ks[14], (nb, ATTN_HEADS, REL_TABLE), 0.1),
        "attn_w_o": nrm(ks[15], (nb, hd, D_MODEL), beta * hd ** -0.5),
        "ffn_w1": nrm(ks[16], (nd, D_MODEL, D_FF), beta * D_MODEL ** -0.5),
        "ffn_w3": nrm(ks[17], (nd, D_MODEL, D_FF), beta * D_MODEL ** -0.5),
        "ffn_w2": nrm(ks[18], (nd, D_FF, D_MODEL), beta * D_FF ** -0.5),
        "moe_router": nrm(ks[19], (nm, D_MODEL, N_EXPERTS), D_MODEL ** -0.5),
        "moe_w1": nrm(ks[20], (nm, N_EXPERTS, D_MODEL, D_FF_EXPERT), beta * D_MODEL ** -0.5),
        "moe_w3": nrm(ks[21], (nm, N_EXPERTS, D_MODEL, D_FF_EXPERT), beta * D_MODEL ** -0.5),
        "moe_w2": nrm(ks[22], (nm, N_EXPERTS, D_FF_EXPERT, D_MODEL), beta * D_FF_EXPERT ** -0.5),
        "ln_g": 1.0 + nrm(ks[23], (DEPTH, 2, D_MODEL), 0.02),
        "ln_b": nrm(ks[24], (DEPTH, 2, D_MODEL), 0.02),
    }


def reference(x, ssm_w_in, ssm_log_step, ssm_lam_re, ssm_lam_im, ssm_b_re, ssm_b_im, ssm_c_re, ssm_c_im,
              ssm_d, ssm_w_glu, attn_w_k, attn_w_v, attn_w_q, attn_rel_bias, attn_w_o,
              ffn_w1, ffn_w3, ffn_w2, moe_router, moe_w1, moe_w3, moe_w2, ln_g, ln_b):
    k_sh = None
    v_sh = None
    for l in range(DEPTH):
        if l < N_A_LAYERS:
            mix = s5_mixer(x, ssm_w_in[l], ssm_log_step[l], ssm_lam_re[l], ssm_lam_im[l],
                           ssm_b_re[l], ssm_b_im[l], ssm_c_re[l], ssm_c_im[l], ssm_d[l], ssm_w_glu[l])
        else:
            j = l - N_A_LAYERS
            mix = chunk_attention(x, k_sh, v_sh, attn_w_q[j], attn_rel_bias[j], attn_w_o[j])
        x = layer_norm(DEEPNORM_ALPHA * x + mix, ln_g[l, 0], ln_b[l, 0])
        if l % 2 == 0:
            f = swiglu_ffn(x, ffn_w1[l // 2], ffn_w3[l // 2], ffn_w2[l // 2])
        else:
            f = moe_ffn(x, moe_router[l // 2], moe_w1[l // 2], moe_w3[l // 2], moe_w2[l // 2])
        x = layer_norm(DEEPNORM_ALPHA * x + f, ln_g[l, 1], ln_b[l, 1])
        if l == N_A_LAYERS - 1:
            k_sh, v_sh = shared_kv(x, attn_w_k, attn_w_v)
    return x
```

```python
import functools
import math

import jax
import jax.numpy as jnp
from jax import lax
from jax.experimental import pallas as pl
from jax.experimental.pallas import tpu as pltpu

F32 = jnp.float32
BF16 = jnp.bfloat16
I32 = jnp.int32

DEPTH = 4
CHUNK = 64
SSM_GROUP = 16
M_SHIFT = SSM_GROUP.bit_length() - 1
SSM_STATE = 64
ATTN_HEADS = 32
LEFT_CHUNKS = 8
REL_CLIP = 128
N_EXPERTS = 8
TOP_K = 2
DEEPNORM_ALPHA = (2.0 * DEPTH) ** 0.25
LN_EPS = 1e-5
MASK_VALUE = -1e30

LANES = 128
VMEM_LIMIT = 56 * 1024 * 1024

S5_T = 64
ROW_TILE = 512
FF_TILE = 512
ATTN_TQ = 256
MOE_TM = 512
ROUTER_TM = 512
DMA_TM = 256


def _cparams(sem, vmem=VMEM_LIMIT):
    return pltpu.CompilerParams(dimension_semantics=sem, vmem_limit_bytes=vmem)


def _dot(a, b):
    return jnp.dot(a, b, preferred_element_type=F32)


def _split3(a):
    hi = a.astype(BF16)
    r = a - hi.astype(F32)
    mid = r.astype(BF16)
    lo = (r - mid.astype(F32)).astype(BF16)
    return hi, mid, lo


def _dot_f32(a, b):
    ah, am, al = _split3(a)
    bh, bm, bl = _split3(b)
    small = _dot(am, bm) + _dot(ah, bl) + _dot(al, bh)
    return _dot(ah, bh) + (_dot(ah, bm) + _dot(am, bh) + small)


def _layer_norm(z, g, b):
    mu = jnp.mean(z, axis=-1, keepdims=True)
    zc = z - mu
    var = jnp.mean(zc * zc, axis=-1, keepdims=True)
    return zc * lax.rsqrt(var + LN_EPS) * g + b


def _sigmoid(x):
    return 1.0 / (1.0 + jnp.exp(-x))


def _gelu_tanh(x):
    c = math.sqrt(2.0 / math.pi)
    return 0.5 * x * (1.0 + jnp.tanh(c * (x + 0.044715 * (x * x * x))))


def _const_spec(shape):
    nd = len(shape)
    return pl.BlockSpec(shape, lambda *_: (0,) * nd, pipeline_mode=pl.Buffered(1))


def _cpow(er, ei, n, nbits):
    pr = jnp.ones(n.shape, F32)
    pi = jnp.zeros(n.shape, F32)
    br, bi = er, ei
    for k in range(nbits):
        bit = ((n >> k) & 1) == 1
        fr = jnp.where(bit, br, 1.0)
        fi = jnp.where(bit, bi, 0.0)
        pr, pi = pr * fr - pi * fi, pr * fi + pi * fr
        br, bi = br * br - bi * bi, 2.0 * br * bi
    return pr, pi


def _s5_param_kernel(ls_ref, lamc_ref, lamr_ref, btr_ref, bti_ref, ccat_ref, a1_ref, a2_ref, d_ref,
                     at_ref, wt_ref, vt_ref, lt_ref, *, T):
    M = SSM_GROUP
    P = lamc_ref.shape[0]
    TM = T * M
    nb = max(1, int(T).bit_length())
    delta = jnp.exp(ls_ref[...])

    lr = lamc_ref[:, 0:1]
    li = lamc_ref[:, 1:2]
    mag = jnp.exp(lr * delta)
    er = mag * jnp.cos(li * delta)
    ei = mag * jnp.sin(li * delta)
    den = lr * lr + li * li
    nr = er - 1.0
    cfr = (nr * lr + ei * li) / den
    cfi = (ei * lr - nr * li) / den
    lane = lax.broadcasted_iota(I32, (P, TM), 1)
    jrev = (T - 1) - (lane >> M_SHIFT)
    pr, pi = _cpow(er, ei, jrev, nb)
    btr = btr_ref[...]
    bti = bti_ref[...]
    bbr = cfr * btr - cfi * bti
    bbi = cfr * bti + cfi * btr
    lbr = pr * bbr - pi * bbi
    lbi = pr * bbi + pi * bbr
    wt = jnp.concatenate([lbr, lbi], axis=0)
    wt_ref[...] = wt.astype(BF16)

    krev = _dot_f32(ccat_ref[...], wt)
    row = lax.broadcasted_iota(I32, (M, TM), 0)
    lane2 = lax.broadcasted_iota(I32, (M, TM), 1)
    krev = krev + jnp.where(lane2 == (T - 1) * M + row, d_ref[...], 0.0)
    kext = jnp.concatenate([krev, jnp.zeros_like(krev)], axis=1)
    for t in range(T):
        off = (T - 1 - t) * M
        win = kext if off == 0 else pltpu.roll(kext, 2 * TM - off, axis=1)
        at_ref[t * M:(t + 1) * M, :] = win[:, :TM].astype(BF16)

    lrr = lamr_ref[0:1, :]
    lir = lamr_ref[1:2, :]
    magr = jnp.exp(lrr * delta)
    err = magr * jnp.cos(lir * delta)
    eir = magr * jnp.sin(lir * delta)
    rowi = lax.broadcasted_iota(I32, (TM, 2 * P), 0)
    qr, qi = _cpow(err, eir, (rowi >> M_SHIFT) + 1, nb)
    vt_ref[...] = (a1_ref[...] * qr + a2_ref[...] * qi).astype(BF16)

    tr, ti = er, ei
    ar = jnp.ones_like(er)
    ai = jnp.zeros_like(ei)
    n = T
    while n:
        if n & 1:
            ar, ai = ar * tr - ai * ti, ar * ti + ai * tr
        tr, ti = tr * tr - ti * ti, 2.0 * tr * ti
        n >>= 1
    lt_ref[...] = jnp.concatenate([ar, ai], axis=0)


def _s5_params(log_step, lam_re, lam_im, b_re, b_im, c_re, c_im, d_skip, T):
    G, P = lam_re.shape
    M = SSM_GROUP
    TM = T * M
    ls = log_step.reshape(G, 1, 1)
    lamc = jnp.stack([lam_re, lam_im], axis=-1)
    lamr = jnp.stack([jnp.tile(lam_re, (1, 2)), jnp.tile(lam_im, (1, 2))], 1)
    btr = jnp.tile(b_re, (1, 1, T))
    bti = jnp.tile(b_im, (1, 1, T))
    ccat = jnp.concatenate([c_re, -c_im], axis=-1)
    ctr = jnp.tile(c_re, (1, T, 1))
    cti = jnp.tile(c_im, (1, T, 1))
    a1 = jnp.concatenate([ctr, -cti], axis=-1)
    a2 = jnp.concatenate([-cti, -ctr], axis=-1)
    dcol = d_skip.reshape(G, M, 1)

    def gspec(*shape):
        nd = len(shape)
        return pl.BlockSpec((None,) + shape, lambda g: (g,) + (0,) * nd)

    return pl.pallas_call(
        functools.partial(_s5_param_kernel, T=T),
        grid=(G,),
        in_specs=[gspec(1, 1), gspec(P, 2), gspec(2, 2 * P), gspec(P, TM), gspec(P, TM),
                  gspec(M, 2 * P), gspec(TM, 2 * P), gspec(TM, 2 * P), gspec(M, 1)],
        out_specs=[gspec(TM, TM), gspec(2 * P, TM), gspec(TM, 2 * P), gspec(2 * P, 1)],
        out_shape=[jax.ShapeDtypeStruct((G, TM, TM), BF16),
                   jax.ShapeDtypeStruct((G, 2 * P, TM), BF16),
                   jax.ShapeDtypeStruct((G, TM, 2 * P), BF16),
                   jax.ShapeDtypeStruct((G, 2 * P, 1), F32)],
        compiler_params=_cparams(("arbitrary",)),
        name="s5_params",
    )(ls, lamc, lamr, btr, bti, ccat, a1, a2, dcol)


def _s5_uproj_kernel(x_ref, w_ref, o_ref):
    nb, nc, d = x_ref.shape
    xs = x_ref[...].reshape(nb * nc, d).astype(BF16)
    ut = lax.dot_general(w_ref[...], xs, (((1,), (1,)), ((), ())), preferred_element_type=F32)
    o_ref[...] = ut.astype(BF16).reshape(o_ref.shape)


def _s5_scan_kernel(ut_ref, at_ref, wt_ref, vt_ref, lt_ref, y_ref, *, NC):
    T, M, BC = ut_ref.shape
    P = lt_ref.shape[0] // 2
    u = ut_ref[...].reshape(T * M, BC)
    z = _dot(wt_ref[...], u)
    hr, hi = z[:P], z[P:]
    dr = lt_ref[:P, :]
    di = lt_ref[P:, :]
    cpos = lax.broadcasted_iota(I32, (P, BC), 1) & (NC - 1)
    sh = 1
    while sh < NC:
        m = cpos >= sh
        sr = jnp.where(m, pltpu.roll(hr, sh, axis=1), 0.0)
        si = jnp.where(m, pltpu.roll(hi, sh, axis=1), 0.0)
        hr, hi = hr + (dr * sr - di * si), hi + (dr * si + di * sr)
        dr, di = dr * dr - di * di, 2.0 * dr * di
        sh *= 2
    m1 = cpos >= 1
    pr = jnp.where(m1, pltpu.roll(hr, 1, axis=1), 0.0)
    pi = jnp.where(m1, pltpu.roll(hi, 1, axis=1), 0.0)
    hp = jnp.concatenate([pr, pi], axis=0).astype(BF16)
    y = _dot(at_ref[...], u) + _dot(vt_ref[...], hp)
    y_ref[...] = _gelu_tanh(y).astype(BF16).reshape(y_ref.shape)


def _s5_out_kernel(y_ref, w_ref, x_ref, g_ref, b_ref, o_ref):
    G, M, bc = y_ref.shape
    nb, nc, d = x_ref.shape
    yt = y_ref[...].reshape(G * M, bc)
    o = lax.dot_general(yt, w_ref[...], (((0,), (0,)), ((), ())), preferred_element_type=F32)
    mix = o[:, :d] * _sigmoid(o[:, d:])
    z = DEEPNORM_ALPHA * x_ref[...].reshape(bc, d) + mix
    o_ref[...] = _layer_norm(z, g_ref[...], b_ref[...]).reshape(o_ref.shape)


def _s5_layer(x, w_in_t, w_glu, ops, ln_g, ln_b, T):
    B, L, D = x.shape
    H = w_in_t.shape[0]
    M = SSM_GROUP
    G = H // M
    NC = L // T
    BC = B * NC
    at, wt, vt, lt = ops
    P2 = wt.shape[1]
    x3 = x.reshape(B, NC, T * D)

    ut = pl.pallas_call(
        _s5_uproj_kernel,
        grid=(T,),
        in_specs=[pl.BlockSpec((B, NC, D), lambda s: (0, 0, s)), _const_spec((H, D))],
        out_specs=pl.BlockSpec((G, None, M, BC), lambda s: (0, s, 0, 0)),
        out_shape=jax.ShapeDtypeStruct((G, T, M, BC), BF16),
        compiler_params=_cparams(("arbitrary",)),
        name="s5_uproj",
    )(x3, w_in_t)

    ys = pl.pallas_call(
        functools.partial(_s5_scan_kernel, NC=NC),
        grid=(G,),
        in_specs=[pl.BlockSpec((None, T, M, BC), lambda g: (g, 0, 0, 0)),
                  pl.BlockSpec((None, T * M, T * M), lambda g: (g, 0, 0)),
                  pl.BlockSpec((None, P2, T * M), lambda g: (g, 0, 0)),
                  pl.BlockSpec((None, T * M, P2), lambda g: (g, 0, 0)),
                  pl.BlockSpec((None, P2, 1), lambda g: (g, 0, 0))],
        out_specs=pl.BlockSpec((None, T, M, BC), lambda g: (g, 0, 0, 0)),
        out_shape=jax.ShapeDtypeStruct((G, T, M, BC), BF16),
        compiler_params=_cparams(("arbitrary",)),
        name="s5_scan",
    )(ut, at, wt, vt, lt)

    nsplit = 2 if (B % 2 == 0 and (BC // 2) % LANES == 0) else 1
    bt = B // nsplit
    bct = BC // nsplit
    out = pl.pallas_call(
        _s5_out_kernel,
        grid=(T, nsplit),
        in_specs=[pl.BlockSpec((G, None, M, bct), lambda s, h: (0, s, 0, h)),
                  _const_spec((H, 2 * D)),
                  pl.BlockSpec((bt, NC, D), lambda s, h: (h, 0, s)),
                  _const_spec((1, D)), _const_spec((1, D))],
        out_specs=pl.BlockSpec((bt, NC, D), lambda s, h: (h, 0, s)),
        out_shape=jax.ShapeDtypeStruct((B, NC, T * D), F32),
        compiler_params=_cparams(("arbitrary", "arbitrary")),
        name="s5_out",
    )(ys, w_glu, x3, ln_g, ln_b)
    return out.reshape(B, L, D)


def _swiglu_step(xb_ref, acc_ref, w1_ref, w3_ref, w2_ref):
    xb = xb_ref[...]
    h1 = _dot(xb, w1_ref[...])
    h3 = _dot(xb, w3_ref[...])
    h = (h1 * _sigmoid(h1) * h3).astype(BF16)
    acc_ref[...] += _dot(h, w2_ref[...])


def _ffn_kernel(x_ref, w1_ref, w3_ref, w2_ref, g_ref, b_ref, o_ref, xb_ref, acc_ref):
    j = pl.program_id(1)

    @pl.when(j == 0)
    def _():
        xb_ref[...] = x_ref[...].astype(BF16)
        acc_ref[...] = jnp.zeros_like(acc_ref)

    _swiglu_step(xb_ref, acc_ref, w1_ref, w3_ref, w2_ref)

    @pl.when(j == pl.num_programs(1) - 1)
    def _():
        z = DEEPNORM_ALPHA * x_ref[...] + acc_ref[...]
        o_ref[...] = _layer_norm(z, g_ref[...], b_ref[...])


def _ffn_layer(x2, w1, w3, w2, ln_g, ln_b, tm=ROW_TILE, tf=FF_TILE):
    N, D = x2.shape
    Fp = w1.shape[1]
    tm = min(tm, N)
    tf = min(tf, Fp)
    return pl.pallas_call(
        _ffn_kernel,
        grid=(N // tm, Fp // tf),
        in_specs=[pl.BlockSpec((tm, D), lambda i, j: (i, 0)),
                  pl.BlockSpec((D, tf), lambda i, j: (0, j)),
                  pl.BlockSpec((D, tf), lambda i, j: (0, j)),
                  pl.BlockSpec((tf, D), lambda i, j: (j, 0)),
                  _const_spec((1, D)), _const_spec((1, D))],
        out_specs=pl.BlockSpec((tm, D), lambda i, j: (i, 0)),
        out_shape=jax.ShapeDtypeStruct((N, D), F32),
        scratch_shapes=[pltpu.VMEM((tm, D), BF16), pltpu.VMEM((tm, D), F32)],
        compiler_params=_cparams(("arbitrary", "arbitrary")),
        name="ffn_dense",
    )(x2, w1, w3, w2, ln_g, ln_b)


def _expert_kernel(be_ref, nb_ref, x_ref, w1_ref, w3_ref, w2_ref, y_ref, xb_ref, acc_ref):
    i = pl.program_id(0)
    j = pl.program_id(1)

    @pl.when(i < nb_ref[0])
    def _():
        @pl.when(j == 0)
        def _():
            xb_ref[...] = x_ref[...].astype(BF16)
            acc_ref[...] = jnp.zeros_like(acc_ref)

        _swiglu_step(xb_ref, acc_ref, w1_ref, w3_ref, w2_ref)

        @pl.when(j == pl.num_programs(1) - 1)
        def _():
            y_ref[...] = acc_ref[...]

    @pl.when(i >= nb_ref[0])
    def _():
        y_ref[...] = jnp.zeros_like(y_ref)


def _expert_ffn(xs, block_exp, nb_used, w1, w3, w2, tm, tf=FF_TILE):
    Pm, D = xs.shape
    E, _, F = w1.shape
    tf = min(tf, F)
    nblk = Pm // tm
    nj = F // tf

    def row_map(i, j, be, nb):
        return (jnp.minimum(i, nb[0] - 1), 0)

    def jj(i, j, nb):
        return jnp.where(i < nb[0], j, nj - 1)

    def w13_map(i, j, be, nb):
        return (be[jnp.minimum(i, nb[0] - 1)], 0, jj(i, j, nb))

    def w2_map(i, j, be, nb):
        return (be[jnp.minimum(i, nb[0] - 1)], jj(i, j, nb), 0)

    grid_spec = pltpu.PrefetchScalarGridSpec(
        num_scalar_prefetch=2,
        grid=(nblk, nj),
        in_specs=[pl.BlockSpec((tm, D), row_map),
                  pl.BlockSpec((None, D, tf), w13_map),
                  pl.BlockSpec((None, D, tf), w13_map),
                  pl.BlockSpec((None, tf, D), w2_map)],
        out_specs=pl.BlockSpec((tm, D), lambda i, j, be, nb: (i, 0)),
        scratch_shapes=[pltpu.VMEM((tm, D), BF16), pltpu.VMEM((tm, D), F32)],
    )
    return pl.pallas_call(
        _expert_kernel,
        grid_spec=grid_spec,
        out_shape=jax.ShapeDtypeStruct((Pm, D), F32),
        compiler_params=_cparams(("arbitrary", "arbitrary")),
        name="moe_experts",
    )(block_exp, nb_used, xs, w1, w3, w2)


def _router_kernel(x_ref, wr_ref, e_ref, gate_ref, rank_ref, cnt_ref, carry_ref, *, E):
    i = pl.program_id(0)

    @pl.when(i == 0)
    def _():
        carry_ref[...] = jnp.zeros_like(carry_ref)

    tm = x_ref.shape[0]
    logits = _dot_f32(x_ref[...], wr_ref[...])
    lane = lax.broadcasted_iota(I32, logits.shape, 1).astype(F32)
    neg = jnp.float32(-3.0e38)
    lg = jnp.where(lane < E, logits, neg)
    m1 = jnp.max(lg, axis=1, keepdims=True)
    i1 = jnp.min(jnp.where(lg == m1, lane, float(LANES)), axis=1, keepdims=True)
    lg2 = jnp.where(lane == i1, neg, lg)
    m2 = jnp.max(lg2, axis=1, keepdims=True)
    i2 = jnp.min(jnp.where(lg2 == m2, lane, float(LANES)), axis=1, keepdims=True)
    t = jnp.exp(m2 - m1)
    g1 = 1.0 / (1.0 + t)
    g2 = t * g1

    oh1 = lane == i1
    oh2 = lane == i2
    sel = jnp.where(oh1 | oh2, 1.0, 0.0)
    rr = lax.broadcasted_iota(I32, (tm, tm), 0)
    cc = lax.broadcasted_iota(I32, (tm, tm), 1)
    lower = jnp.where(rr > cc, 1.0, 0.0).astype(BF16)
    before = _dot(lower, sel.astype(BF16)) + carry_ref[...]
    r1 = jnp.sum(jnp.where(oh1, before, 0.0), axis=1, keepdims=True)
    r2 = jnp.sum(jnp.where(oh2, before, 0.0), axis=1, keepdims=True)
    carry_ref[...] += jnp.sum(sel, axis=0, keepdims=True)

    e_ref[:, 0:1] = i1.astype(I32)
    e_ref[:, 1:2] = i2.astype(I32)
    gate_ref[:, 0:1] = g1
    gate_ref[:, 1:2] = g2
    rank_ref[:, 0:1] = r1.astype(I32)
    rank_ref[:, 1:2] = r2.astype(I32)
    cnt_ref[...] = carry_ref[...].astype(I32)


def _router(x2, w_router):
    N, D = x2.shape
    E = w_router.shape[1]
    tm = min(ROUTER_TM, N)
    wr = jnp.pad(w_router, ((0, 0), (0, LANES - E)))
    return pl.pallas_call(
        functools.partial(_router_kernel, E=E),
        grid=(N // tm,),
        in_specs=[pl.BlockSpec((tm, D), lambda i: (i, 0)), _const_spec((D, LANES))],
        out_specs=[pl.BlockSpec((tm, TOP_K), lambda i: (i, 0)),
                   pl.BlockSpec((tm, TOP_K), lambda i: (i, 0)),
                   pl.BlockSpec((tm, TOP_K), lambda i: (i, 0)),
                   pl.BlockSpec((1, LANES), lambda i: (0, 0))],
        out_shape=[jax.ShapeDtypeStruct((N, TOP_K), I32),
                   jax.ShapeDtypeStruct((N, TOP_K), F32),
                   jax.ShapeDtypeStruct((N, TOP_K), I32),
                   jax.ShapeDtypeStruct((1, LANES), I32)],
        scratch_shapes=[pltpu.VMEM((1, LANES), F32)],
        compiler_params=_cparams(("arbitrary",)),
        name="moe_router",
    )(x2, wr)


def _row_copy(src, src_row, dst, dst_row, sem):
    return pltpu.make_async_copy(src.at[pl.ds(src_row, 1), :], dst.at[pl.ds(dst_row, 1), :], sem)


def _dispatch_kernel(dest_ref, x_hbm, xs_in_hbm, xs_hbm, sem, *, tm):
    del xs_in_hbm
    base = pl.program_id(0) * tm

    def issue(r, c):
        for k in range(TOP_K):
            _row_copy(x_hbm, base + r, xs_hbm, dest_ref[TOP_K * r + k], sem).start()
        return c

    lax.fori_loop(0, tm, issue, 0)

    def drain(r, c):
        for k in range(TOP_K):
            _row_copy(x_hbm, base + r, xs_hbm, dest_ref[TOP_K * r + k], sem).wait()
        return c

    lax.fori_loop(0, tm, drain, 0)


def _dispatch(x2, dest_flat, n_slots):
    N, D = x2.shape
    tm = min(DMA_TM, N)
    xs0 = jnp.zeros((n_slots, D), F32)
    return pl.pallas_call(
        functools.partial(_dispatch_kernel, tm=tm),
        grid=(N // tm,),
        in_specs=[pl.BlockSpec((TOP_K * tm,), lambda i: (i,), memory_space=pltpu.SMEM),
                  pl.BlockSpec(memory_space=pl.ANY),
                  pl.BlockSpec(memory_space=pl.ANY)],
        out_specs=pl.BlockSpec(memory_space=pl.ANY),
        out_shape=jax.ShapeDtypeStruct((n_slots, D), F32),
        scratch_shapes=[pltpu.SemaphoreType.DMA(())],
        input_output_aliases={2: 0},
        compiler_params=pltpu.CompilerParams(dimension_semantics=("arbitrary",), has_side_effects=True),
        name="moe_dispatch",
    )(dest_flat, x2, xs0)


def _combine_kernel(dest_ref, ys_hbm, x_ref, gate_ref, g_ref, b_ref, o_ref, buf_ref, sem, *, tm):
    def issue(r, c):
        for k in range(TOP_K):
            _row_copy(ys_hbm, dest_ref[TOP_K * r + k], buf_ref.at[k], r, sem).start()
        return c

    lax.fori_loop(0, tm, issue, 0)

    def drain(r, c):
        for k in range(TOP_K):
            _row_copy(ys_hbm, dest_ref[TOP_K * r + k], buf_ref.at[k], r, sem).wait()
        return c

    lax.fori_loop(0, tm, drain, 0)
    f = gate_ref[:, 0:1] * buf_ref[0] + gate_ref[:, 1:2] * buf_ref[1]
    z = DEEPNORM_ALPHA * x_ref[...] + f
    o_ref[...] = _layer_norm(z, g_ref[...], b_ref[...])


def _combine(ys, dest_flat, x2, gates, ln_g, ln_b):
    N, D = x2.shape
    tm = min(DMA_TM, N)
    return pl.pallas_call(
        functools.partial(_combine_kernel, tm=tm),
        grid=(N // tm,),
        in_specs=[pl.BlockSpec((TOP_K * tm,), lambda i: (i,), memory_space=pltpu.SMEM),
                  pl.BlockSpec(memory_space=pl.ANY),
                  pl.BlockSpec((tm, D), lambda i: (i, 0)),
                  pl.BlockSpec((tm, TOP_K), lambda i: (i, 0)),
                  _const_spec((1, D)), _const_spec((1, D))],
        out_specs=pl.BlockSpec((tm, D), lambda i: (i, 0)),
        out_shape=jax.ShapeDtypeStruct((N, D), F32),
        scratch_shapes=[pltpu.VMEM((TOP_K, tm, D), F32), pltpu.SemaphoreType.DMA(())],
        compiler_params=_cparams(("arbitrary",)),
        name="moe_combine",
    )(dest_flat, ys, x2, gates, ln_g, ln_b)


def _moe_layer(x2, w_router, w1, w3, w2, ln_g, ln_b, tm=MOE_TM):
    N, D = x2.shape
    E = w_router.shape[1]
    eidx, gates, rank, cnt = _router(x2, w_router)
    counts = cnt[0, :E]
    padded = ((counts + tm - 1) // tm) * tm
    pend = jnp.cumsum(padded)
    pstart = pend - padded
    n_assign = N * TOP_K
    nblk = -(-n_assign // tm) + E
    n_slots = nblk * tm
    dest = (pstart[eidx] + rank).reshape(-1).astype(I32)
    block_exp = jnp.minimum(
        jnp.searchsorted(pend, jnp.arange(nblk, dtype=I32) * tm, side="right"), E - 1).astype(I32)
    nb_used = (pend[-1] // tm).astype(I32).reshape(1)
    xs = _dispatch(x2, dest, n_slots)
    ys = _expert_ffn(xs, block_exp, nb_used, w1, w3, w2, tm)
    return _combine(ys, dest, x2, gates, ln_g, ln_b)


def _kv_kernel(x_ref, wk_ref, wv_ref, k_ref, v_ref, *, pad_blocks):
    i = pl.program_id(1)

    @pl.when(i < pad_blocks)
    def _():
        k_ref[...] = jnp.zeros_like(k_ref)
        v_ref[...] = jnp.zeros_like(v_ref)

    @pl.when(i >= pad_blocks)
    def _():
        xb = x_ref[...].astype(BF16)
        k_ref[...] = _dot(xb, wk_ref[...]).astype(BF16)
        v_ref[...] = _dot(xb, wv_ref[...]).astype(BF16)


def _shared_kv(x, w_k, w_v, pad_rows):
    B, L, D = x.shape
    HD = w_k.shape[1]
    tm = min(ROW_TILE, pad_rows)
    pad_blocks = pad_rows // tm
    spec_o = pl.BlockSpec((None, tm, HD), lambda b, i: (b, i, 0))
    return pl.pallas_call(
        functools.partial(_kv_kernel, pad_blocks=pad_blocks),
        grid=(B, (L + pad_rows) // tm),
        in_specs=[pl.BlockSpec((None, tm, D), lambda b, i: (b, jnp.maximum(i - pad_blocks, 0), 0)),
                  _const_spec((D, HD)), _const_spec((D, HD))],
        out_specs=[spec_o, spec_o],
        out_shape=[jax.ShapeDtypeStruct((B, L + pad_rows, HD), BF16)] * 2,
        compiler_params=_cparams(("arbitrary", "arbitrary")),
        name="attn_kv",
    )(x, w_k, w_v)


def _qproj_kernel(x_ref, w_ref, o_ref, *, scale):
    o_ref[...] = (_dot(x_ref[...].astype(BF16), w_ref[...]) * scale).astype(o_ref.dtype)


def _q_proj(x2, w_q, scale):
    N, D = x2.shape
    HD = w_q.shape[1]
    tm = min(ROW_TILE, N)
    return pl.pallas_call(
        functools.partial(_qproj_kernel, scale=scale),
        grid=(N // tm,),
        in_specs=[pl.BlockSpec((tm, D), lambda i: (i, 0)), _const_spec((D, HD))],
        out_specs=pl.BlockSpec((tm, HD), lambda i: (i, 0)),
        out_shape=jax.ShapeDtypeStruct((N, HD), BF16),
        compiler_params=_cparams(("arbitrary",)),
        name="attn_q",
    )(x2, w_q)


def _attn_kernel(q_ref, k_ref, v_ref, bias_ref, o_ref, *, pad_rows):
    tq = q_ref.shape[0]
    wk = tq + pad_rows
    dh = q_ref.shape[1] // 2
    start = pl.multiple_of(pl.program_id(2) * tq, tq)
    kwin = k_ref[pl.ds(start, wk), :]
    vwin = v_ref[pl.ds(start, wk), :]
    q2 = q_ref[...]
    lane = lax.broadcasted_iota(I32, q2.shape, 1)
    krow = lax.broadcasted_iota(I32, (tq, wk), 1) + start
    real = krow >= pad_rows
    outs = []
    for h in range(2):
        in_head = (lane >= h * dh) & (lane < (h + 1) * dh)
        qh = jnp.where(in_head, q2, jnp.zeros_like(q2))
        s = lax.dot_general(qh, kwin, (((1,), (1,)), ((), ())), preferred_element_type=F32)
        s = jnp.where(real, s + bias_ref[h], MASK_VALUE)
        m = jnp.max(s, axis=1, keepdims=True)
        p = jnp.exp(s - m)
        l = jnp.sum(p, axis=1, keepdims=True)
        outs.append(_dot(p.astype(BF16), vwin) / l)
    o_ref[...] = jnp.where(lane < dh, outs[0], outs[1]).astype(o_ref.dtype)


def _attn_bias(rel_bias, tq, pad_rows):
    wk = tq + pad_rows
    qi = jnp.arange(tq)[:, None]
    kj = jnp.arange(wk)[None, :]
    dist = qi + pad_rows - kj
    idx = jnp.clip(dist, -(CHUNK - 1), REL_CLIP) + (CHUNK - 1)
    qc = qi // CHUNK
    kc = kj // CHUNK
    band = (kc >= qc) & (kc <= qc + LEFT_CHUNKS)
    bias = jnp.where(band[None], rel_bias[:, idx].astype(F32), MASK_VALUE)
    return bias.reshape(rel_bias.shape[0] // 2, 2, tq, wk)


def _chunk_attention(q, kp, vp, bias, pad_rows):
    B, L, HD = q.shape
    tq = bias.shape[2]
    wk = tq + pad_rows
    npair = HD // LANES
    lp = kp.shape[1]
    return pl.pallas_call(
        functools.partial(_attn_kernel, pad_rows=pad_rows),
        grid=(npair, B, L // tq),
        in_specs=[pl.BlockSpec((None, tq, LANES), lambda p, b, i: (b, i, p)),
                  pl.BlockSpec((None, lp, LANES), lambda p, b, i: (b, 0, p)),
                  pl.BlockSpec((None, lp, LANES), lambda p, b, i: (b, 0, p)),
                  pl.BlockSpec((None, 2, tq, wk), lambda p, b, i: (p, 0, 0, 0))],
        out_specs=pl.BlockSpec((None, tq, LANES), lambda p, b, i: (b, i, p)),
        out_shape=jax.ShapeDtypeStruct((B, L, HD), BF16),
        compiler_params=_cparams(("arbitrary", "arbitrary", "arbitrary")),
        name="attn_core",
    )(q, kp, vp, bias)


def _oproj_kernel(a_ref, w_ref, x_ref, g_ref, b_ref, o_ref):
    z = DEEPNORM_ALPHA * x_ref[...] + _dot(a_ref[...], w_ref[...])
    o_ref[...] = _layer_norm(z, g_ref[...], b_ref[...])


def _o_proj(a2, w_o, x2, ln_g, ln_b):
    N, D = x2.shape
    HD = a2.shape[1]
    tm = min(ROW_TILE, N)
    return pl.pallas_call(
        _oproj_kernel,
        grid=(N // tm,),
        in_specs=[pl.BlockSpec((tm, HD), lambda i: (i, 0)), _const_spec((HD, D)),
                  pl.BlockSpec((tm, D), lambda i: (i, 0)), _const_spec((1, D)), _const_spec((1, D))],
        out_specs=pl.BlockSpec((tm, D), lambda i: (i, 0)),
        out_shape=jax.ShapeDtypeStruct((N, D), F32),
        compiler_params=_cparams(("arbitrary",)),
        name="attn_o",
    )(a2, w_o, x2, ln_g, ln_b)


def _attn_layer(x, kp, vp, w_q, rel_bias, w_o, ln_g, ln_b, pad_rows):
    B, L, D = x.shape
    HD = w_q.shape[1]
    dh = HD // rel_bias.shape[0]
    tq = min(ATTN_TQ, L)
    x2 = x.reshape(B * L, D)
    q = _q_proj(x2, w_q, dh ** -0.5).reshape(B, L, HD)
    bias = _attn_bias(rel_bias, tq, pad_rows)
    o = _chunk_attention(q, kp, vp, bias, pad_rows)
    return _o_proj(o.reshape(B * L, HD), w_o, x2, ln_g, ln_b).reshape(B, L, D)


def _pad_ff(w1, w3, w2, tf):
    F = w1.shape[1]
    Fp = -(-F // tf) * tf
    w1 = jnp.pad(w1, ((0, 0), (0, Fp - F))).astype(BF16)
    w3 = jnp.pad(w3, ((0, 0), (0, Fp - F))).astype(BF16)
    w2 = jnp.pad(w2, ((0, Fp - F), (0, 0))).astype(BF16)
    return w1, w3, w2


def kernel(x, ssm_w_in, ssm_log_step, ssm_lam_re, ssm_lam_im, ssm_b_re, ssm_b_im, ssm_c_re, ssm_c_im,
           ssm_d, ssm_w_glu, attn_w_k, attn_w_v, attn_w_q, attn_rel_bias, attn_w_o,
           ffn_w1, ffn_w3, ffn_w2, moe_router, moe_w1, moe_w3, moe_w2, ln_g, ln_b):
    B, L, D = x.shape
    n_a = ssm_w_in.shape[0]
    depth = ln_g.shape[0]
    T = min(S5_T, L)
    pad_rows = LEFT_CHUNKS * CHUNK
    kp = vp = None
    for l in range(depth):
        g0, b0 = ln_g[l, 0].reshape(1, D), ln_b[l, 0].reshape(1, D)
        g1, b1 = ln_g[l, 1].reshape(1, D), ln_b[l, 1].reshape(1, D)
        if l < n_a:
            ops = _s5_params(ssm_log_step[l], ssm_lam_re[l], ssm_lam_im[l], ssm_b_re[l], ssm_b_im[l],
                             ssm_c_re[l], ssm_c_im[l], ssm_d[l], T)
            x = _s5_layer(x, ssm_w_in[l].T.astype(BF16), ssm_w_glu[l].astype(BF16), ops, g0, b0, T)
        else:
            j = l - n_a
            x = _attn_layer(x, kp, vp, attn_w_q[j].astype(BF16), attn_rel_bias[j],
                            attn_w_o[j].astype(BF16), g0, b0, pad_rows)
        x2 = x.reshape(B * L, D)
        if l % 2 == 0:
            w1, w3, w2 = _pad_ff(ffn_w1[l // 2], ffn_w3[l // 2], ffn_w2[l // 2], FF_TILE)
            x2 = _ffn_layer(x2, w1, w3, w2, g1, b1)
        else:
            x2 = _moe_layer(x2, moe_router[l // 2], moe_w1[l // 2].astype(BF16),
                            moe_w3[l // 2].astype(BF16), moe_w2[l // 2].astype(BF16), g1, b1)
        x = x2.reshape(B, L, D)
        if l == n_a - 1:
            kp, vp = _shared_kv(x, attn_w_k.astype(BF16), attn_w_v.astype(BF16), pad_rows)
    return x
```

```python
import functools
import math

import jax
import jax.numpy as jnp
from jax import lax
from jax.experimental import pallas as pl
from jax.experimental.pallas import tpu as pltpu

F32 = jnp.float32
BF16 = jnp.bfloat16
I32 = jnp.int32

DEPTH = 4
CHUNK = 64
SSM_GROUP = 16
M_SHIFT = SSM_GROUP.bit_length() - 1
SSM_STATE = 64
ATTN_HEADS = 32
LEFT_CHUNKS = 8
REL_CLIP = 128
N_EXPERTS = 8
TOP_K = 2
DEEPNORM_ALPHA = (2.0 * DEPTH) ** 0.25
LN_EPS = 1e-5
MASK_VALUE = -1e30

LANES = 128
VMEM_LIMIT = 56 * 1024 * 1024

S5_T = 64
ROW_TILE = 512
FF_TILE = 512
MOE_FF_TILE = 1024
ATTN_TQ = 256
ATTN_PAIRS = 4
MOE_TM = 512
ROUTER_TM = 512
DMA_TM = 256


def _cparams(sem, vmem=VMEM_LIMIT):
    return pltpu.CompilerParams(dimension_semantics=sem, vmem_limit_bytes=vmem)


def _dot(a, b):
    return jnp.dot(a, b, preferred_element_type=F32)


def _split3(a):
    hi = a.astype(BF16)
    r = a - hi.astype(F32)
    mid = r.astype(BF16)
    lo = (r - mid.astype(F32)).astype(BF16)
    return hi, mid, lo


def _dot_f32(a, b):
    ah, am, al = _split3(a)
    bh, bm, bl = _split3(b)
    small = _dot(am, bm) + _dot(ah, bl) + _dot(al, bh)
    return _dot(ah, bh) + (_dot(ah, bm) + _dot(am, bh) + small)


def _layer_norm(z, g, b):
    mu = jnp.mean(z, axis=-1, keepdims=True)
    zc = z - mu
    var = jnp.mean(zc * zc, axis=-1, keepdims=True)
    return zc * lax.rsqrt(var + LN_EPS) * g + b


def _sigmoid(x):
    return 1.0 / (1.0 + jnp.exp(-x))


def _gelu_tanh(x):
    c = math.sqrt(2.0 / math.pi)
    return 0.5 * x * (1.0 + jnp.tanh(c * (x + 0.044715 * (x * x * x))))


def _const_spec(shape):
    nd = len(shape)
    return pl.BlockSpec(shape, lambda *_: (0,) * nd, pipeline_mode=pl.Buffered(1))


def _layer_spec(shape, l):
    nd = len(shape)
    return pl.BlockSpec((None,) + tuple(shape), lambda *_: (l,) + (0,) * nd, pipeline_mode=pl.Buffered(1))


def _cpow(er, ei, n, nbits):
    pr = jnp.ones(n.shape, F32)
    pi = jnp.zeros(n.shape, F32)
    br, bi = er, ei
    for k in range(nbits):
        bit = ((n >> k) & 1) == 1
        fr = jnp.where(bit, br, 1.0)
        fi = jnp.where(bit, bi, 0.0)
        pr, pi = pr * fr - pi * fi, pr * fi + pi * fr
        br, bi = br * br - bi * bi, 2.0 * br * bi
    return pr, pi


def _s5_param_kernel(ls_ref, lamc_ref, lamr_ref, btr_ref, bti_ref, ccat_ref, a1_ref, a2_ref, d_ref,
                     at_ref, wt_ref, vt_ref, lt_ref, *, T):
    M = SSM_GROUP
    P = lamc_ref.shape[0]
    TM = T * M
    nb = max(1, int(T).bit_length())
    delta = jnp.exp(ls_ref[...])

    lr = lamc_ref[:, 0:1]
    li = lamc_ref[:, 1:2]
    mag = jnp.exp(lr * delta)
    er = mag * jnp.cos(li * delta)
    ei = mag * jnp.sin(li * delta)
    den = lr * lr + li * li
    nr = er - 1.0
    cfr = (nr * lr + ei * li) / den
    cfi = (ei * lr - nr * li) / den
    lane = lax.broadcasted_iota(I32, (P, TM), 1)
    jrev = (T - 1) - (lane >> M_SHIFT)
    pr, pi = _cpow(er, ei, jrev, nb)
    btr = btr_ref[...]
    bti = bti_ref[...]
    bbr = cfr * btr - cfi * bti
    bbi = cfr * bti + cfi * btr
    lbr = pr * bbr - pi * bbi
    lbi = pr * bbi + pi * bbr
    wt = jnp.concatenate([lbr, lbi], axis=0)
    wt_ref[...] = wt.astype(BF16)

    krev = _dot_f32(ccat_ref[...], wt)
    row = lax.broadcasted_iota(I32, (M, TM), 0)
    lane2 = lax.broadcasted_iota(I32, (M, TM), 1)
    krev = krev + jnp.where(lane2 == (T - 1) * M + row, d_ref[...], 0.0)
    kext = jnp.concatenate([krev, jnp.zeros_like(krev)], axis=1)
    for t in range(T):
        off = (T - 1 - t) * M
        win = kext if off == 0 else pltpu.roll(kext, 2 * TM - off, axis=1)
        at_ref[t * M:(t + 1) * M, :] = win[:, :TM].astype(BF16)

    lrr = lamr_ref[0:1, :]
    lir = lamr_ref[1:2, :]
    magr = jnp.exp(lrr * delta)
    err = magr * jnp.cos(lir * delta)
    eir = magr * jnp.sin(lir * delta)
    rowi = lax.broadcasted_iota(I32, (TM, 2 * P), 0)
    qr, qi = _cpow(err, eir, (rowi >> M_SHIFT) + 1, nb)
    vt_ref[...] = (a1_ref[...] * qr + a2_ref[...] * qi).astype(BF16)

    tr, ti = er, ei
    ar = jnp.ones_like(er)
    ai = jnp.zeros_like(ei)
    n = T
    while n:
        if n & 1:
            ar, ai = ar * tr - ai * ti, ar * ti + ai * tr
        tr, ti = tr * tr - ti * ti, 2.0 * tr * ti
        n >>= 1
    lt_ref[...] = jnp.concatenate([ar, ai], axis=0)


def _s5_params(log_step, lam_re, lam_im, b_re, b_im, c_re, c_im, d_skip, T):
    G, P = lam_re.shape
    M = SSM_GROUP
    TM = T * M
    ls = log_step.reshape(G, 1, 1)
    lamc = jnp.stack([lam_re, lam_im], axis=-1)
    lamr = jnp.stack([jnp.tile(lam_re, (1, 2)), jnp.tile(lam_im, (1, 2))], 1)
    btr = jnp.tile(b_re, (1, 1, T))
    bti = jnp.tile(b_im, (1, 1, T))
    ccat = jnp.concatenate([c_re, -c_im], axis=-1)
    ctr = jnp.tile(c_re, (1, T, 1))
    cti = jnp.tile(c_im, (1, T, 1))
    a1 = jnp.concatenate([ctr, -cti], axis=-1)
    a2 = jnp.concatenate([-cti, -ctr], axis=-1)
    dcol = d_skip.reshape(G, M, 1)

    def gspec(*shape):
        nd = len(shape)
        return pl.BlockSpec((None,) + shape, lambda g: (g,) + (0,) * nd)

    return pl.pallas_call(
        functools.partial(_s5_param_kernel, T=T),
        grid=(G,),
        in_specs=[gspec(1, 1), gspec(P, 2), gspec(2, 2 * P), gspec(P, TM), gspec(P, TM),
                  gspec(M, 2 * P), gspec(TM, 2 * P), gspec(TM, 2 * P), gspec(M, 1)],
        out_specs=[gspec(TM, TM), gspec(2 * P, TM), gspec(TM, 2 * P), gspec(2 * P, 1)],
        out_shape=[jax.ShapeDtypeStruct((G, TM, TM), BF16),
                   jax.ShapeDtypeStruct((G, 2 * P, TM), BF16),
                   jax.ShapeDtypeStruct((G, TM, 2 * P), BF16),
                   jax.ShapeDtypeStruct((G, 2 * P, 1), F32)],
        compiler_params=_cparams(("arbitrary",)),
        name="s5_params",
    )(ls, lamc, lamr, btr, bti, ccat, a1, a2, dcol)


def _s5_uproj_kernel(x_ref, w_ref, o_ref):
    xs = x_ref[...].astype(BF16)
    ut = lax.dot_general(w_ref[...], xs, (((1,), (1,)), ((), ())), preferred_element_type=F32)
    o_ref[...] = ut.astype(BF16).reshape(o_ref.shape)


def _s5_scan_kernel(ut_ref, at_ref, wt_ref, vt_ref, lt_ref, y_ref, *, NC):
    T, M, BC = ut_ref.shape
    P = lt_ref.shape[0] // 2
    u = ut_ref[...].reshape(T * M, BC)
    z = _dot(wt_ref[...], u)
    hr, hi = z[:P], z[P:]
    dr = lt_ref[:P, :]
    di = lt_ref[P:, :]
    cpos = lax.broadcasted_iota(I32, (P, BC), 1) & (NC - 1)
    sh = 1
    while sh < NC:
        m = cpos >= sh
        sr = jnp.where(m, pltpu.roll(hr, sh, axis=1), 0.0)
        si = jnp.where(m, pltpu.roll(hi, sh, axis=1), 0.0)
        hr, hi = hr + (dr * sr - di * si), hi + (dr * si + di * sr)
        dr, di = dr * dr - di * di, 2.0 * dr * di
        sh *= 2
    m1 = cpos >= 1
    pr = jnp.where(m1, pltpu.roll(hr, 1, axis=1), 0.0)
    pi = jnp.where(m1, pltpu.roll(hi, 1, axis=1), 0.0)
    hp = jnp.concatenate([pr, pi], axis=0).astype(BF16)
    y = _dot(at_ref[...], u) + _dot(vt_ref[...], hp)
    y_ref[...] = _gelu_tanh(y).astype(BF16).reshape(y_ref.shape)


def _s5_out_kernel(y_ref, w_ref, x_ref, g_ref, b_ref, o_ref):
    G, M, bc = y_ref.shape
    d = x_ref.shape[1]
    yt = y_ref[...].reshape(G * M, bc)
    o = lax.dot_general(yt, w_ref[...], (((0,), (0,)), ((), ())), preferred_element_type=F32)
    mix = o[:, :d] * _sigmoid(o[:, d:])
    z = DEEPNORM_ALPHA * x_ref[...] + mix
    o_ref[...] = _layer_norm(z, g_ref[...], b_ref[...])


def _s5_layer(x, w_in_t, w_glu, l, ops, ln_g, ln_b, T):
    B, L, D = x.shape
    H = w_in_t.shape[1]
    M = SSM_GROUP
    G = H // M
    NC = L // T
    BC = B * NC
    at, wt, vt, lt = ops
    P2 = wt.shape[1]
    x3 = x.reshape(BC, T * D)

    ut = pl.pallas_call(
        _s5_uproj_kernel,
        grid=(T,),
        in_specs=[pl.BlockSpec((BC, D), lambda s: (0, s)), _layer_spec((H, D), l)],
        out_specs=pl.BlockSpec((G, None, M, BC), lambda s: (0, s, 0, 0)),
        out_shape=jax.ShapeDtypeStruct((G, T, M, BC), BF16),
        compiler_params=_cparams(("arbitrary",)),
        name="s5_uproj",
    )(x3, w_in_t)

    ys = pl.pallas_call(
        functools.partial(_s5_scan_kernel, NC=NC),
        grid=(G,),
        in_specs=[pl.BlockSpec((None, T, M, BC), lambda g: (g, 0, 0, 0)),
                  pl.BlockSpec((None, T * M, T * M), lambda g: (g, 0, 0)),
                  pl.BlockSpec((None, P2, T * M), lambda g: (g, 0, 0)),
                  pl.BlockSpec((None, T * M, P2), lambda g: (g, 0, 0)),
                  pl.BlockSpec((None, P2, 1), lambda g: (g, 0, 0))],
        out_specs=pl.BlockSpec((None, T, M, BC), lambda g: (g, 0, 0, 0)),
        out_shape=jax.ShapeDtypeStruct((G, T, M, BC), BF16),
        compiler_params=_cparams(("arbitrary",)),
        name="s5_scan",
    )(ut, at, wt, vt, lt)

    nsplit = 2 if (BC // 2) % LANES == 0 else 1
    bct = BC // nsplit
    out = pl.pallas_call(
        _s5_out_kernel,
        grid=(T, nsplit),
        in_specs=[pl.BlockSpec((G, None, M, bct), lambda s, h: (0, s, 0, h)),
                  _layer_spec((H, 2 * D), l),
                  pl.BlockSpec((bct, D), lambda s, h: (h, s)),
                  _const_spec((1, D)), _const_spec((1, D))],
        out_specs=pl.BlockSpec((bct, D), lambda s, h: (h, s)),
        out_shape=jax.ShapeDtypeStruct((BC, T * D), F32),
        compiler_params=_cparams(("arbitrary", "arbitrary")),
        name="s5_out",
    )(ys, w_glu, x3, ln_g, ln_b)
    return out.reshape(B, L, D)


def _swiglu_step(xb_ref, acc_ref, w1_ref, w3_ref, w2_ref):
    xb = xb_ref[...]
    h1 = _dot(xb, w1_ref[...])
    h3 = _dot(xb, w3_ref[...])
    h = (h1 * _sigmoid(h1) * h3).astype(BF16)
    acc_ref[...] += _dot(h, w2_ref[...])


def _ffn_kernel(x_ref, w1_ref, w3_ref, w2_ref, g_ref, b_ref, o_ref, xb_ref, acc_ref):
    j = pl.program_id(1)

    @pl.when(j == 0)
    def _():
        xb_ref[...] = x_ref[...].astype(BF16)
        acc_ref[...] = jnp.zeros_like(acc_ref)

    _swiglu_step(xb_ref, acc_ref, w1_ref, w3_ref, w2_ref)

    @pl.when(j == pl.num_programs(1) - 1)
    def _():
        z = DEEPNORM_ALPHA * x_ref[...] + acc_ref[...]
        o_ref[...] = _layer_norm(z, g_ref[...], b_ref[...])


def _ffn_layer(x2, w1, w3, w2, l, ln_g, ln_b, tm=ROW_TILE, tf=FF_TILE):
    N, D = x2.shape
    Fp = w1.shape[2]
    tm = min(tm, N)
    tf = min(tf, Fp)
    return pl.pallas_call(
        _ffn_kernel,
        grid=(N // tm, Fp // tf),
        in_specs=[pl.BlockSpec((tm, D), lambda i, j: (i, 0)),
                  pl.BlockSpec((None, D, tf), lambda i, j: (l, 0, j)),
                  pl.BlockSpec((None, D, tf), lambda i, j: (l, 0, j)),
                  pl.BlockSpec((None, tf, D), lambda i, j: (l, j, 0)),
                  _const_spec((1, D)), _const_spec((1, D))],
        out_specs=pl.BlockSpec((tm, D), lambda i, j: (i, 0)),
        out_shape=jax.ShapeDtypeStruct((N, D), F32),
        scratch_shapes=[pltpu.VMEM((tm, D), BF16), pltpu.VMEM((tm, D), F32)],
        compiler_params=_cparams(("arbitrary", "arbitrary")),
        name="ffn_dense",
    )(x2, w1, w3, w2, ln_g, ln_b)


def _expert_kernel(be_ref, nb_ref, x_ref, w1_ref, w3_ref, w2_ref, y_ref, xb_ref):
    i = pl.program_id(0)
    j = pl.program_id(1)

    @pl.when(j == 0)
    def _():
        y_ref[...] = jnp.zeros_like(y_ref)

    @pl.when(i < nb_ref[0])
    def _():
        @pl.when(j == 0)
        def _():
            xb_ref[...] = x_ref[...].astype(BF16)

        _swiglu_step(xb_ref, y_ref, w1_ref, w3_ref, w2_ref)


def _expert_ffn(xs, block_exp, nb_used, w1, w3, w2, l, tm, tf=None):
    Pm, D = xs.shape
    _, E, _, F = w1.shape
    tf = MOE_FF_TILE if tf is None else tf
    tf = min(tf, F)
    nblk = Pm // tm
    nj = F // tf

    def row_map(i, j, be, nb):
        return (jnp.minimum(i, nb[0] - 1), 0)

    def jj(i, j, nb):
        return jnp.where(i < nb[0], j, nj - 1)

    def w13_map(i, j, be, nb):
        return (l, be[jnp.minimum(i, nb[0] - 1)], 0, jj(i, j, nb))

    def w2_map(i, j, be, nb):
        return (l, be[jnp.minimum(i, nb[0] - 1)], jj(i, j, nb), 0)

    grid_spec = pltpu.PrefetchScalarGridSpec(
        num_scalar_prefetch=2,
        grid=(nblk, nj),
        in_specs=[pl.BlockSpec((tm, D), row_map),
                  pl.BlockSpec((None, None, D, tf), w13_map),
                  pl.BlockSpec((None, None, D, tf), w13_map),
                  pl.BlockSpec((None, None, tf, D), w2_map)],
        out_specs=pl.BlockSpec((tm, D), lambda i, j, be, nb: (i, 0)),
        scratch_shapes=[pltpu.VMEM((tm, D), BF16)],
    )
    return pl.pallas_call(
        _expert_kernel,
        grid_spec=grid_spec,
        out_shape=jax.ShapeDtypeStruct((Pm, D), F32),
        compiler_params=_cparams(("arbitrary", "arbitrary")),
        name="moe_experts",
    )(block_exp, nb_used, xs, w1, w3, w2)


def _router_kernel(x_ref, wr_ref, e_ref, gate_ref, rank_ref, cnt_ref, carry_ref, *, E):
    i = pl.program_id(0)

    @pl.when(i == 0)
    def _():
        carry_ref[...] = jnp.zeros_like(carry_ref)

    tm = x_ref.shape[0]
    logits = _dot_f32(x_ref[...], wr_ref[...])
    lane = lax.broadcasted_iota(I32, logits.shape, 1).astype(F32)
    neg = jnp.float32(-3.0e38)
    lg = jnp.where(lane < E, logits, neg)
    m1 = jnp.max(lg, axis=1, keepdims=True)
    i1 = jnp.min(jnp.where(lg == m1, lane, float(LANES)), axis=1, keepdims=True)
    lg2 = jnp.where(lane == i1, neg, lg)
    m2 = jnp.max(lg2, axis=1, keepdims=True)
    i2 = jnp.min(jnp.where(lg2 == m2, lane, float(LANES)), axis=1, keepdims=True)
    t = jnp.exp(m2 - m1)
    g1 = 1.0 / (1.0 + t)
    g2 = t * g1

    oh1 = lane == i1
    oh2 = lane == i2
    sel = jnp.where(oh1 | oh2, 1.0, 0.0)
    rr = lax.broadcasted_iota(I32, (tm, tm), 0)
    cc = lax.broadcasted_iota(I32, (tm, tm), 1)
    lower = jnp.where(rr > cc, 1.0, 0.0).astype(BF16)
    before = _dot(lower, sel.astype(BF16)) + carry_ref[...]
    r1 = jnp.sum(jnp.where(oh1, before, 0.0), axis=1, keepdims=True)
    r2 = jnp.sum(jnp.where(oh2, before, 0.0), axis=1, keepdims=True)
    carry_ref[...] += jnp.sum(sel, axis=0, keepdims=True)

    e_ref[:, 0:1] = i1.astype(I32)
    e_ref[:, 1:2] = i2.astype(I32)
    gate_ref[:, 0:1] = g1
    gate_ref[:, 1:2] = g2
    rank_ref[:, 0:1] = r1.astype(I32)
    rank_ref[:, 1:2] = r2.astype(I32)
    cnt_ref[...] = carry_ref[...].astype(I32)


def _router(x2, w_router):
    N, D = x2.shape
    E = w_router.shape[1]
    tm = min(ROUTER_TM, N)
    wr = jnp.pad(w_router, ((0, 0), (0, LANES - E)))
    return pl.pallas_call(
        functools.partial(_router_kernel, E=E),
        grid=(N // tm,),
        in_specs=[pl.BlockSpec((tm, D), lambda i: (i, 0)), _const_spec((D, LANES))],
        out_specs=[pl.BlockSpec((tm, TOP_K), lambda i: (i, 0)),
                   pl.BlockSpec((tm, TOP_K), lambda i: (i, 0)),
                   pl.BlockSpec((tm, TOP_K), lambda i: (i, 0)),
                   pl.BlockSpec((1, LANES), lambda i: (0, 0))],
        out_shape=[jax.ShapeDtypeStruct((N, TOP_K), I32),
                   jax.ShapeDtypeStruct((N, TOP_K), F32),
                   jax.ShapeDtypeStruct((N, TOP_K), I32),
                   jax.ShapeDtypeStruct((1, LANES), I32)],
        scratch_shapes=[pltpu.VMEM((1, LANES), F32)],
        compiler_params=_cparams(("arbitrary",)),
        name="moe_router",
    )(x2, wr)


def _row_copy(src, src_row, dst, dst_row, sem):
    return pltpu.make_async_copy(src.at[pl.ds(src_row, 1), :], dst.at[pl.ds(dst_row, 1), :], sem)


def _dispatch_kernel(dest_ref, x_ref, xs_in_hbm, xs_hbm, sem, *, tm):
    del xs_in_hbm

    def issue(r, c):
        for k in range(TOP_K):
            _row_copy(x_ref, r, xs_hbm, dest_ref[TOP_K * r + k], sem).start()
        return c

    lax.fori_loop(0, tm, issue, 0)

    def drain(r, c):
        for k in range(TOP_K):
            _row_copy(x_ref, r, xs_hbm, dest_ref[TOP_K * r + k], sem).wait()
        return c

    lax.fori_loop(0, tm, drain, 0)


def _dispatch(x2, dest_flat, n_slots):
    N, D = x2.shape
    tm = min(DMA_TM, N)
    xs0 = jnp.zeros((n_slots, D), F32)
    return pl.pallas_call(
        functools.partial(_dispatch_kernel, tm=tm),
        grid=(N // tm,),
        in_specs=[pl.BlockSpec((TOP_K * tm,), lambda i: (i,), memory_space=pltpu.SMEM),
                  pl.BlockSpec((tm, D), lambda i: (i, 0)),
                  pl.BlockSpec(memory_space=pl.ANY)],
        out_specs=pl.BlockSpec(memory_space=pl.ANY),
        out_shape=jax.ShapeDtypeStruct((n_slots, D), F32),
        scratch_shapes=[pltpu.SemaphoreType.DMA(())],
        input_output_aliases={2: 0},
        compiler_params=pltpu.CompilerParams(dimension_semantics=("arbitrary",), has_side_effects=True),
        name="moe_dispatch",
    )(dest_flat, x2, xs0)


def _combine_kernel(dest_ref, ys_hbm, x_ref, gate_ref, g_ref, b_ref, o_ref, buf_ref, sem, *, tm):
    def issue(r, c):
        for k in range(TOP_K):
            _row_copy(ys_hbm, dest_ref[TOP_K * r + k], buf_ref.at[k], r, sem).start()
        return c

    lax.fori_loop(0, tm, issue, 0)

    def drain(r, c):
        for k in range(TOP_K):
            _row_copy(ys_hbm, dest_ref[TOP_K * r + k], buf_ref.at[k], r, sem).wait()
        return c

    lax.fori_loop(0, tm, drain, 0)
    f = gate_ref[:, 0:1] * buf_ref[0] + gate_ref[:, 1:2] * buf_ref[1]
    z = DEEPNORM_ALPHA * x_ref[...] + f
    o_ref[...] = _layer_norm(z, g_ref[...], b_ref[...])


def _combine(ys, dest_flat, x2, gates, ln_g, ln_b):
    N, D = x2.shape
    tm = min(DMA_TM, N)
    return pl.pallas_call(
        functools.partial(_combine_kernel, tm=tm),
        grid=(N // tm,),
        in_specs=[pl.BlockSpec((TOP_K * tm,), lambda i: (i,), memory_space=pltpu.SMEM),
                  pl.BlockSpec(memory_space=pl.ANY),
                  pl.BlockSpec((tm, D), lambda i: (i, 0)),
                  pl.BlockSpec((tm, TOP_K), lambda i: (i, 0)),
                  _const_spec((1, D)), _const_spec((1, D))],
        out_specs=pl.BlockSpec((tm, D), lambda i: (i, 0)),
        out_shape=jax.ShapeDtypeStruct((N, D), F32),
        scratch_shapes=[pltpu.VMEM((TOP_K, tm, D), F32), pltpu.SemaphoreType.DMA(())],
        compiler_params=_cparams(("arbitrary",)),
        name="moe_combine",
    )(dest_flat, ys, x2, gates, ln_g, ln_b)


def _moe_layer(x2, w_router, w1, w3, w2, l, ln_g, ln_b, tm=MOE_TM):
    N, D = x2.shape
    E = w_router.shape[1]
    eidx, gates, rank, cnt = _router(x2, w_router)
    counts = cnt[0, :E]
    padded = ((counts + tm - 1) // tm) * tm
    pend = jnp.cumsum(padded)
    pstart = pend - padded
    n_assign = N * TOP_K
    nblk = -(-n_assign // tm) + E
    n_slots = nblk * tm
    dest = (pstart[eidx] + rank).reshape(-1).astype(I32)
    block_exp = jnp.minimum(
        jnp.searchsorted(pend, jnp.arange(nblk, dtype=I32) * tm, side="right"), E - 1).astype(I32)
    nb_used = (pend[-1] // tm).astype(I32).reshape(1)
    xs = _dispatch(x2, dest, n_slots)
    ys = _expert_ffn(xs, block_exp, nb_used, w1, w3, w2, l, tm)
    return _combine(ys, dest, x2, gates, ln_g, ln_b)


def _store_head_pairs(o_ref, r):
    for p in range(o_ref.shape[0]):
        o_ref[p] = r[:, p * LANES:(p + 1) * LANES].astype(o_ref.dtype)


def _kv_kernel(x_ref, wk_ref, wv_ref, k_ref, v_ref, *, pad_blocks):
    i = pl.program_id(1)

    @pl.when(i < pad_blocks)
    def _():
        k_ref[...] = jnp.zeros_like(k_ref)
        v_ref[...] = jnp.zeros_like(v_ref)

    @pl.when(i >= pad_blocks)
    def _():
        xb = x_ref[...].astype(BF16)
        _store_head_pairs(k_ref, _dot(xb, wk_ref[...]))
        _store_head_pairs(v_ref, _dot(xb, wv_ref[...]))


def _shared_kv(x, w_k, w_v, pad_rows):
    B, L, D = x.shape
    HD = w_k.shape[1]
    npair = HD // LANES
    tm = min(ROW_TILE, pad_rows)
    pad_blocks = pad_rows // tm
    spec_o = pl.BlockSpec((None, npair, tm, LANES), lambda b, i: (b, 0, i, 0))
    return pl.pallas_call(
        functools.partial(_kv_kernel, pad_blocks=pad_blocks),
        grid=(B, (L + pad_rows) // tm),
        in_specs=[pl.BlockSpec((None, tm, D), lambda b, i: (b, jnp.maximum(i - pad_blocks, 0), 0)),
                  _const_spec((D, HD)), _const_spec((D, HD))],
        out_specs=[spec_o, spec_o],
        out_shape=[jax.ShapeDtypeStruct((B, npair, L + pad_rows, LANES), BF16)] * 2,
        compiler_params=_cparams(("arbitrary", "arbitrary")),
        name="attn_kv",
    )(x, w_k, w_v)


def _qproj_kernel(x_ref, w_ref, o_ref, *, scale):
    _store_head_pairs(o_ref, _dot(x_ref[...].astype(BF16), w_ref[...]) * scale)


def _q_proj(x, w_q, l, scale):
    B, L, D = x.shape
    HD = w_q.shape[2]
    npair = HD // LANES
    tm = min(ROW_TILE, L)
    return pl.pallas_call(
        functools.partial(_qproj_kernel, scale=scale),
        grid=(B, L // tm),
        in_specs=[pl.BlockSpec((None, tm, D), lambda b, i: (b, i, 0)), _layer_spec((D, HD), l)],
        out_specs=pl.BlockSpec((None, npair, tm, LANES), lambda b, i: (b, 0, i, 0)),
        out_shape=jax.ShapeDtypeStruct((B, npair, L, LANES), BF16),
        compiler_params=_cparams(("arbitrary", "arbitrary")),
        name="attn_q",
    )(x, w_q)


def _attn_kernel(q_ref, k_ref, v_ref, bias_ref, o_ref, *, pad_rows):
    npg, tq, _ = q_ref.shape
    wk = tq + pad_rows
    dh = LANES // 2
    start = pl.multiple_of(pl.program_id(2) * tq, tq)
    lane = lax.broadcasted_iota(I32, (tq, LANES), 1)
    krow = lax.broadcasted_iota(I32, (tq, wk), 1) + start
    real = krow >= pad_rows
    for pp in range(npg):
        kwin = k_ref[pp, pl.ds(start, wk), :]
        vwin = v_ref[pp, pl.ds(start, wk), :]
        q2 = q_ref[pp]
        outs = []
        for h in range(2):
            in_head = (lane >= h * dh) & (lane < (h + 1) * dh)
            qh = jnp.where(in_head, q2, jnp.zeros_like(q2))
            s = lax.dot_general(qh, kwin, (((1,), (1,)), ((), ())), preferred_element_type=F32)
            s = jnp.where(real, s + bias_ref[pp, h], MASK_VALUE)
            m = jnp.max(s, axis=1, keepdims=True)
            p = jnp.exp(s - m)
            l = jnp.sum(p, axis=1, keepdims=True)
            outs.append(_dot(p.astype(BF16), vwin) / l)
        o_ref[pp] = jnp.where(lane < dh, outs[0], outs[1]).astype(o_ref.dtype)


def _attn_bias(rel_bias, tq, pad_rows):
    wk = tq + pad_rows
    j = jnp.arange(tq - 1 + wk)
    idx = jnp.clip(tq - 1 + pad_rows - j, -(CHUNK - 1), REL_CLIP) + (CHUNK - 1)
    ext = rel_bias[:, idx].astype(F32)
    rows = jax.vmap(lambda s: lax.dynamic_slice_in_dim(ext, s, wk, axis=1))(tq - 1 - jnp.arange(tq))
    toep = jnp.swapaxes(rows, 0, 1)
    qc = jnp.arange(tq)[:, None] // CHUNK
    kc = jnp.arange(wk)[None, :] // CHUNK
    band = (kc >= qc) & (kc <= qc + LEFT_CHUNKS)
    bias = jnp.where(band[None], toep, MASK_VALUE)
    return bias.reshape(rel_bias.shape[0] // 2, 2, tq, wk)


def _chunk_attention(q, kp, vp, bias, pad_rows):
    B, npair, L, _ = q.shape
    tq = bias.shape[2]
    wk = tq + pad_rows
    npg = min(ATTN_PAIRS, npair)
    lp = kp.shape[2]
    return pl.pallas_call(
        functools.partial(_attn_kernel, pad_rows=pad_rows),
        grid=(npair // npg, B, L // tq),
        in_specs=[pl.BlockSpec((None, npg, tq, LANES), lambda g, b, i: (b, g, i, 0)),
                  pl.BlockSpec((None, npg, lp, LANES), lambda g, b, i: (b, g, 0, 0)),
                  pl.BlockSpec((None, npg, lp, LANES), lambda g, b, i: (b, g, 0, 0)),
                  pl.BlockSpec((npg, 2, tq, wk), lambda g, b, i: (g, 0, 0, 0))],
        out_specs=pl.BlockSpec((None, npg, tq, LANES), lambda g, b, i: (b, g, i, 0)),
        out_shape=jax.ShapeDtypeStruct((B, npair, L, LANES), BF16),
        compiler_params=_cparams(("arbitrary", "arbitrary", "arbitrary")),
        name="attn_core",
    )(q, kp, vp, bias)


def _oproj_kernel(a_ref, w_ref, x_ref, g_ref, b_ref, o_ref):
    a = jnp.concatenate([a_ref[p] for p in range(a_ref.shape[0])], axis=1)
    z = DEEPNORM_ALPHA * x_ref[...] + _dot(a, w_ref[...])
    o_ref[...] = _layer_norm(z, g_ref[...], b_ref[...])


def _o_proj(a, w_o, l, x, ln_g, ln_b):
    B, L, D = x.shape
    npair = a.shape[1]
    HD = npair * LANES
    tm = min(ROW_TILE, L)
    return pl.pallas_call(
        _oproj_kernel,
        grid=(B, L // tm),
        in_specs=[pl.BlockSpec((None, npair, tm, LANES), lambda b, i: (b, 0, i, 0)),
                  _layer_spec((HD, D), l),
                  pl.BlockSpec((None, tm, D), lambda b, i: (b, i, 0)),
                  _const_spec((1, D)), _const_spec((1, D))],
        out_specs=pl.BlockSpec((None, tm, D), lambda b, i: (b, i, 0)),
        out_shape=jax.ShapeDtypeStruct((B, L, D), F32),
        compiler_params=_cparams(("arbitrary", "arbitrary")),
        name="attn_o",
    )(a, w_o, x, ln_g, ln_b)


def _attn_layer(x, kp, vp, w_q, rel_bias, w_o, l, ln_g, ln_b, pad_rows):
    B, L, D = x.shape
    HD = w_q.shape[2]
    dh = HD // rel_bias.shape[0]
    tq = min(ATTN_TQ, L)
    q = _q_proj(x, w_q, l, dh ** -0.5)
    bias = _attn_bias(rel_bias, tq, pad_rows)
    o = _chunk_attention(q, kp, vp, bias, pad_rows)
    return _o_proj(o, w_o, l, x, ln_g, ln_b)


def _pad_ff(w1, w3, w2, tf):
    F = w1.shape[2]
    Fp = -(-F // tf) * tf
    w1 = jnp.pad(w1, ((0, 0), (0, 0), (0, Fp - F))).astype(BF16)
    w3 = jnp.pad(w3, ((0, 0), (0, 0), (0, Fp - F))).astype(BF16)
    w2 = jnp.pad(w2, ((0, 0), (0, Fp - F), (0, 0))).astype(BF16)
    return w1, w3, w2


def kernel(x, ssm_w_in, ssm_log_step, ssm_lam_re, ssm_lam_im, ssm_b_re, ssm_b_im, ssm_c_re, ssm_c_im,
           ssm_d, ssm_w_glu, attn_w_k, attn_w_v, attn_w_q, attn_rel_bias, attn_w_o,
           ffn_w1, ffn_w3, ffn_w2, moe_router, moe_w1, moe_w3, moe_w2, ln_g, ln_b):
    B, L, D = x.shape
    n_a = ssm_w_in.shape[0]
    depth = ln_g.shape[0]
    T = min(S5_T, L)
    pad_rows = LEFT_CHUNKS * CHUNK
    w_in_t = jnp.swapaxes(ssm_w_in, 1, 2).astype(BF16)
    w_glu = ssm_w_glu.astype(BF16)
    w_q, w_o = attn_w_q.astype(BF16), attn_w_o.astype(BF16)
    f1, f3, f2 = _pad_ff(ffn_w1, ffn_w3, ffn_w2, FF_TILE)
    m1, m3, m2 = moe_w1.astype(BF16), moe_w3.astype(BF16), moe_w2.astype(BF16)
    kp = vp = None
    for l in range(depth):
        g0, b0 = ln_g[l, 0].reshape(1, D), ln_b[l, 0].reshape(1, D)
        g1, b1 = ln_g[l, 1].reshape(1, D), ln_b[l, 1].reshape(1, D)
        if l < n_a:
            ops = _s5_params(ssm_log_step[l], ssm_lam_re[l], ssm_lam_im[l], ssm_b_re[l], ssm_b_im[l],
                             ssm_c_re[l], ssm_c_im[l], ssm_d[l], T)
            x = _s5_layer(x, w_in_t, w_glu, l, ops, g0, b0, T)
        else:
            x = _attn_layer(x, kp, vp, w_q, attn_rel_bias[l - n_a], w_o, l - n_a, g0, b0, pad_rows)
        x2 = x.reshape(B * L, D)
        if l % 2 == 0:
            x2 = _ffn_layer(x2, f1, f3, f2, l // 2, g1, b1)
        else:
            x2 = _moe_layer(x2, moe_router[l // 2], m1, m3, m2, l // 2, g1, b1)
        x = x2.reshape(B, L, D)
        if l == n_a - 1:
            kp, vp = _shared_kv(x, attn_w_k.astype(BF16), attn_w_v.astype(BF16), pad_rows)
    return x
```

```python
import functools
import math

import jax
import jax.numpy as jnp
from jax import lax
from jax.experimental import pallas as pl
from jax.experimental.pallas import tpu as pltpu

F32 = jnp.float32
BF16 = jnp.bfloat16
I32 = jnp.int32

DEPTH = 4
CHUNK = 64
SSM_GROUP = 16
M_SHIFT = SSM_GROUP.bit_length() - 1
SSM_STATE = 64
ATTN_HEADS = 32
LEFT_CHUNKS = 8
REL_CLIP = 128
N_EXPERTS = 8
TOP_K = 2
DEEPNORM_ALPHA = (2.0 * DEPTH) ** 0.25
LN_EPS = 1e-5
MASK_VALUE = -1e30
LOG2E = math.log2(math.e)

LANES = 128
VMEM_LIMIT = 56 * 1024 * 1024

S5_T = 64
ROW_TILE = 512
FFN_TM = 512
FF_TILE = 512
MOE_FF_TILE = 1024
ATTN_TQ = 256
ATTN_PAIRS = 4
MOE_TM = 512
ROUTER_TM = 512
DMA_TM = 256
DMA_UNROLL = 8


def _cparams(sem, vmem=VMEM_LIMIT):
    return pltpu.CompilerParams(dimension_semantics=sem, vmem_limit_bytes=vmem)


def _dot(a, b):
    return jnp.dot(a, b, preferred_element_type=F32)


def _split3(a):
    hi = a.astype(BF16)
    r = a - hi.astype(F32)
    mid = r.astype(BF16)
    lo = (r - mid.astype(F32)).astype(BF16)
    return hi, mid, lo


def _dot_f32(a, b):
    ah, am, al = _split3(a)
    bh, bm, bl = _split3(b)
    small = _dot(am, bm) + _dot(ah, bl) + _dot(al, bh)
    return _dot(ah, bh) + (_dot(ah, bm) + _dot(am, bh) + small)


def _copy_dot(a, b, split_left):
    if split_left:
        return sum(_dot(piece, b) for piece in _split3(a))
    return sum(_dot(a, piece) for piece in _split3(b))


def _layer_norm(z, g, b):
    mu = jnp.mean(z, axis=-1, keepdims=True)
    zc = z - mu
    var = jnp.mean(zc * zc, axis=-1, keepdims=True)
    return zc * lax.rsqrt(var + LN_EPS) * g + b


def _sigmoid(x):
    return 1.0 / (1.0 + jnp.exp(-x))


def _gelu_tanh(x):
    c = math.sqrt(2.0 / math.pi)
    return 0.5 * x * (1.0 + jnp.tanh(c * (x + 0.044715 * (x * x * x))))


def _const_spec(shape):
    nd = len(shape)
    return pl.BlockSpec(shape, lambda *_: (0,) * nd, pipeline_mode=pl.Buffered(1))


def _layer_spec(shape, l):
    nd = len(shape)
    return pl.BlockSpec((None,) + tuple(shape), lambda *_: (l,) + (0,) * nd, pipeline_mode=pl.Buffered(1))


def _cpow(er, ei, n, nbits):
    pr = jnp.ones(n.shape, F32)
    pi = jnp.zeros(n.shape, F32)
    br, bi = er, ei
    for k in range(nbits):
        bit = ((n >> k) & 1) == 1
        fr = jnp.where(bit, br, 1.0)
        fi = jnp.where(bit, bi, 0.0)
        pr, pi = pr * fr - pi * fi, pr * fi + pi * fr
        br, bi = br * br - bi * bi, 2.0 * br * bi
    return pr, pi


def _s5_param_kernel(ls_ref, lamc_ref, lamr_ref, btr_ref, bti_ref, ccat_ref, a1_ref, a2_ref, d_ref,
                     at_ref, wt_ref, vt_ref, lt_ref, *, T):
    M = SSM_GROUP
    P = lamc_ref.shape[0]
    TM = T * M
    nb = max(1, int(T).bit_length())
    delta = jnp.exp(ls_ref[...])

    lr = lamc_ref[:, 0:1]
    li = lamc_ref[:, 1:2]
    mag = jnp.exp(lr * delta)
    er = mag * jnp.cos(li * delta)
    ei = mag * jnp.sin(li * delta)
    den = lr * lr + li * li
    nr = er - 1.0
    cfr = (nr * lr + ei * li) / den
    cfi = (ei * lr - nr * li) / den
    jrev = (T - 1) - lax.broadcasted_iota(I32, (P, T), 1)
    pr_t, pi_t = _cpow(er, ei, jrev, nb)
    slot = lax.broadcasted_iota(I32, (T, TM), 1) >> M_SHIFT
    spread = jnp.where(slot == lax.broadcasted_iota(I32, (T, TM), 0), 1.0, 0.0).astype(BF16)
    pr = _copy_dot(pr_t, spread, split_left=True)
    pi = _copy_dot(pi_t, spread, split_left=True)
    btr = btr_ref[...]
    bti = bti_ref[...]
    bbr = cfr * btr - cfi * bti
    bbi = cfr * bti + cfi * btr
    lbr = pr * bbr - pi * bbi
    lbi = pr * bbi + pi * bbr
    wt = jnp.concatenate([lbr, lbi], axis=0)
    wt_ref[...] = wt.astype(BF16)

    krev = _dot_f32(ccat_ref[...], wt)
    row = lax.broadcasted_iota(I32, (M, TM), 0)
    lane2 = lax.broadcasted_iota(I32, (M, TM), 1)
    krev = krev + jnp.where(lane2 == (T - 1) * M + row, d_ref[...], 0.0)
    kext = jnp.concatenate([krev, jnp.zeros_like(krev)], axis=1)
    for t in range(T):
        off = (T - 1 - t) * M
        win = kext if off == 0 else pltpu.roll(kext, 2 * TM - off, axis=1)
        at_ref[t * M:(t + 1) * M, :] = win[:, :TM].astype(BF16)

    lrr = lamr_ref[0:1, :]
    lir = lamr_ref[1:2, :]
    magr = jnp.exp(lrr * delta)
    err = magr * jnp.cos(lir * delta)
    eir = magr * jnp.sin(lir * delta)
    qr_t, qi_t = _cpow(err, eir, lax.broadcasted_iota(I32, (T, 2 * P), 0) + 1, nb)
    rslot = lax.broadcasted_iota(I32, (TM, T), 0) >> M_SHIFT
    rspread = jnp.where(rslot == lax.broadcasted_iota(I32, (TM, T), 1), 1.0, 0.0).astype(BF16)
    qr = _copy_dot(rspread, qr_t, split_left=False)
    qi = _copy_dot(rspread, qi_t, split_left=False)
    vt_ref[...] = (a1_ref[...] * qr + a2_ref[...] * qi).astype(BF16)

    tr, ti = er, ei
    ar = jnp.ones_like(er)
    ai = jnp.zeros_like(ei)
    n = T
    while n:
        if n & 1:
            ar, ai = ar * tr - ai * ti, ar * ti + ai * tr
        tr, ti = tr * tr - ti * ti, 2.0 * tr * ti
        n >>= 1
    lt_ref[...] = jnp.concatenate([ar, ai], axis=0)


def _s5_params(log_step, lam_re, lam_im, b_re, b_im, c_re, c_im, d_skip, T):
    G, P = lam_re.shape
    M = SSM_GROUP
    TM = T * M
    ls = log_step.reshape(G, 1, 1)
    lamc = jnp.stack([lam_re, lam_im], axis=-1)
    lamr = jnp.stack([jnp.tile(lam_re, (1, 2)), jnp.tile(lam_im, (1, 2))], 1)
    btr = jnp.tile(b_re, (1, 1, T))
    bti = jnp.tile(b_im, (1, 1, T))
    ccat = jnp.concatenate([c_re, -c_im], axis=-1)
    ctr = jnp.tile(c_re, (1, T, 1))
    cti = jnp.tile(c_im, (1, T, 1))
    a1 = jnp.concatenate([ctr, -cti], axis=-1)
    a2 = jnp.concatenate([-cti, -ctr], axis=-1)
    dcol = d_skip.reshape(G, M, 1)

    def gspec(*shape):
        nd = len(shape)
        return pl.BlockSpec((None,) + shape, lambda g: (g,) + (0,) * nd)

    return pl.pallas_call(
        functools.partial(_s5_param_kernel, T=T),
        grid=(G,),
        in_specs=[gspec(1, 1), gspec(P, 2), gspec(2, 2 * P), gspec(P, TM), gspec(P, TM),
                  gspec(M, 2 * P), gspec(TM, 2 * P), gspec(TM, 2 * P), gspec(M, 1)],
        out_specs=[gspec(TM, TM), gspec(2 * P, TM), gspec(TM, 2 * P), gspec(2 * P, 1)],
        out_shape=[jax.ShapeDtypeStruct((G, TM, TM), BF16),
                   jax.ShapeDtypeStruct((G, 2 * P, TM), BF16),
                   jax.ShapeDtypeStruct((G, TM, 2 * P), BF16),
                   jax.ShapeDtypeStruct((G, 2 * P, 1), F32)],
        compiler_params=_cparams(("arbitrary",)),
        name="s5_params",
    )(ls, lamc, lamr, btr, bti, ccat, a1, a2, dcol)


def _s5_uproj_kernel(x_ref, w_ref, o_ref):
    xs = x_ref[...].astype(BF16)
    ut = lax.dot_general(w_ref[...], xs, (((1,), (1,)), ((), ())), preferred_element_type=F32)
    o_ref[...] = ut.astype(BF16).reshape(o_ref.shape)


def _s5_scan_kernel(ut_ref, at_ref, wt_ref, vt_ref, lt_ref, y_ref, *, NC):
    T, M, BC = ut_ref.shape
    P = lt_ref.shape[0] // 2
    u = ut_ref[...].reshape(T * M, BC)
    z = _dot(wt_ref[...], u)
    hr, hi = z[:P], z[P:]
    dr = lt_ref[:P, :]
    di = lt_ref[P:, :]
    cpos = lax.broadcasted_iota(I32, (P, BC), 1) & (NC - 1)
    sh = 1
    while sh < NC:
        m = cpos >= sh
        sr = jnp.where(m, pltpu.roll(hr, sh, axis=1), 0.0)
        si = jnp.where(m, pltpu.roll(hi, sh, axis=1), 0.0)
        hr, hi = hr + (dr * sr - di * si), hi + (dr * si + di * sr)
        dr, di = dr * dr - di * di, 2.0 * dr * di
        sh *= 2
    m1 = cpos >= 1
    pr = jnp.where(m1, pltpu.roll(hr, 1, axis=1), 0.0)
    pi = jnp.where(m1, pltpu.roll(hi, 1, axis=1), 0.0)
    hp = jnp.concatenate([pr, pi], axis=0).astype(BF16)
    y = _dot(at_ref[...], u) + _dot(vt_ref[...], hp)
    y_ref[...] = _gelu_tanh(y).astype(BF16).reshape(y_ref.shape)


def _s5_out_kernel(y_ref, w_ref, x_ref, g_ref, b_ref, o_ref):
    G, M, bc = y_ref.shape
    d = x_ref.shape[1]
    yt = y_ref[...].reshape(G * M, bc)
    o = lax.dot_general(yt, w_ref[...], (((0,), (0,)), ((), ())), preferred_element_type=F32)
    mix = o[:, :d] * _sigmoid(o[:, d:])
    z = DEEPNORM_ALPHA * x_ref[...] + mix
    o_ref[...] = _layer_norm(z, g_ref[...], b_ref[...])


def _s5_layer(x, w_in_t, w_glu, l, ops, ln_g, ln_b, T):
    B, L, D = x.shape
    H = w_in_t.shape[1]
    M = SSM_GROUP
    G = H // M
    NC = L // T
    BC = B * NC
    at, wt, vt, lt = ops
    P2 = wt.shape[1]
    x3 = x.reshape(BC, T * D)

    ut = pl.pallas_call(
        _s5_uproj_kernel,
        grid=(T,),
        in_specs=[pl.BlockSpec((BC, D), lambda s: (0, s)), _layer_spec((H, D), l)],
        out_specs=pl.BlockSpec((G, None, M, BC), lambda s: (0, s, 0, 0)),
        out_shape=jax.ShapeDtypeStruct((G, T, M, BC), BF16),
        compiler_params=_cparams(("arbitrary",)),
        name="s5_uproj",
    )(x3, w_in_t)

    ys = pl.pallas_call(
        functools.partial(_s5_scan_kernel, NC=NC),
        grid=(G,),
        in_specs=[pl.BlockSpec((None, T, M, BC), lambda g: (g, 0, 0, 0)),
                  pl.BlockSpec((None, T * M, T * M), lambda g: (g, 0, 0)),
                  pl.BlockSpec((None, P2, T * M), lambda g: (g, 0, 0)),
                  pl.BlockSpec((None, T * M, P2), lambda g: (g, 0, 0)),
                  pl.BlockSpec((None, P2, 1), lambda g: (g, 0, 0))],
        out_specs=pl.BlockSpec((None, T, M, BC), lambda g: (g, 0, 0, 0)),
        out_shape=jax.ShapeDtypeStruct((G, T, M, BC), BF16),
        compiler_params=_cparams(("arbitrary",)),
        name="s5_scan",
    )(ut, at, wt, vt, lt)

    nsplit = 2 if (BC // 2) % LANES == 0 else 1
    bct = BC // nsplit
    out = pl.pallas_call(
        _s5_out_kernel,
        grid=(T, nsplit),
        in_specs=[pl.BlockSpec((G, None, M, bct), lambda s, h: (0, s, 0, h)),
                  _layer_spec((H, 2 * D), l),
                  pl.BlockSpec((bct, D), lambda s, h: (h, s)),
                  _const_spec((1, D)), _const_spec((1, D))],
        out_specs=pl.BlockSpec((bct, D), lambda s, h: (h, s)),
        out_shape=jax.ShapeDtypeStruct((BC, T * D), F32),
        compiler_params=_cparams(("arbitrary", "arbitrary")),
        name="s5_out",
    )(ys, w_glu, x3, ln_g, ln_b)
    return out.reshape(B, L, D)


def _swiglu_step(xb_ref, acc_ref, w1_ref, w3_ref, w2_ref):
    xb = xb_ref[...]
    h1 = _dot(xb, w1_ref[...])
    h3 = _dot(xb, w3_ref[...])
    h = (h1 * _sigmoid(h1) * h3).astype(BF16)
    acc_ref[...] += _dot(h, w2_ref[...])


def _ffn_kernel(x_ref, w1_ref, w3_ref, w2_ref, g_ref, b_ref, o_ref, xb_ref):
    j = pl.program_id(1)

    @pl.when(j == 0)
    def _():
        xb_ref[...] = x_ref[...].astype(BF16)
        o_ref[...] = jnp.zeros_like(o_ref)

    _swiglu_step(xb_ref, o_ref, w1_ref, w3_ref, w2_ref)

    @pl.when(j == pl.num_programs(1) - 1)
    def _():
        z = DEEPNORM_ALPHA * x_ref[...] + o_ref[...]
        o_ref[...] = _layer_norm(z, g_ref[...], b_ref[...])


def _ffn_layer(x2, w1, w3, w2, l, ln_g, ln_b, tm=None, tf=FF_TILE):
    N, D = x2.shape
    Fp = w1.shape[2]
    tm = min(FFN_TM if tm is None else tm, N)
    tf = min(tf, Fp)
    return pl.pallas_call(
        _ffn_kernel,
        grid=(N // tm, Fp // tf),
        in_specs=[pl.BlockSpec((tm, D), lambda i, j: (i, 0)),
                  pl.BlockSpec((None, D, tf), lambda i, j: (l, 0, j)),
                  pl.BlockSpec((None, D, tf), lambda i, j: (l, 0, j)),
                  pl.BlockSpec((None, tf, D), lambda i, j: (l, j, 0)),
                  _const_spec((1, D)), _const_spec((1, D))],
        out_specs=pl.BlockSpec((tm, D), lambda i, j: (i, 0)),
        out_shape=jax.ShapeDtypeStruct((N, D), F32),
        scratch_shapes=[pltpu.VMEM((tm, D), BF16)],
        compiler_params=_cparams(("arbitrary", "arbitrary")),
        name="ffn_dense",
    )(x2, w1, w3, w2, ln_g, ln_b)


def _expert_kernel(be_ref, nb_ref, x_ref, w1_ref, w3_ref, w2_ref, y_ref, xb_ref):
    i = pl.program_id(0)
    j = pl.program_id(1)

    @pl.when(j == 0)
    def _():
        y_ref[...] = jnp.zeros_like(y_ref)

    @pl.when(i < nb_ref[0])
    def _():
        @pl.when(j == 0)
        def _():
            xb_ref[...] = x_ref[...].astype(BF16)

        _swiglu_step(xb_ref, y_ref, w1_ref, w3_ref, w2_ref)


def _expert_ffn(xs, block_exp, nb_used, w1, w3, w2, l, tm, tf=None):
    Pm, D = xs.shape
    _, E, _, F = w1.shape
    tf = MOE_FF_TILE if tf is None else tf
    tf = min(tf, F)
    nblk = Pm // tm
    nj = F // tf

    def row_map(i, j, be, nb):
        return (jnp.minimum(i, nb[0] - 1), 0)

    def jj(i, j, nb):
        return jnp.where(i < nb[0], j, nj - 1)

    def w13_map(i, j, be, nb):
        return (l, be[jnp.minimum(i, nb[0] - 1)], 0, jj(i, j, nb))

    def w2_map(i, j, be, nb):
        return (l, be[jnp.minimum(i, nb[0] - 1)], jj(i, j, nb), 0)

    grid_spec = pltpu.PrefetchScalarGridSpec(
        num_scalar_prefetch=2,
        grid=(nblk, nj),
        in_specs=[pl.BlockSpec((tm, D), row_map),
                  pl.BlockSpec((None, None, D, tf), w13_map),
                  pl.BlockSpec((None, None, D, tf), w13_map),
                  pl.BlockSpec((None, None, tf, D), w2_map)],
        out_specs=pl.BlockSpec((tm, D), lambda i, j, be, nb: (i, 0)),
        scratch_shapes=[pltpu.VMEM((tm, D), BF16)],
    )
    return pl.pallas_call(
        _expert_kernel,
        grid_spec=grid_spec,
        out_shape=jax.ShapeDtypeStruct((Pm, D), F32),
        compiler_params=_cparams(("arbitrary", "arbitrary")),
        name="moe_experts",
    )(block_exp, nb_used, xs, w1, w3, w2)


def _router_kernel(x_ref, wr_ref, e_ref, gate_ref, rank_ref, cnt_ref, carry_ref, *, E):
    i = pl.program_id(0)

    @pl.when(i == 0)
    def _():
        carry_ref[...] = jnp.zeros_like(carry_ref)

    tm = x_ref.shape[0]
    logits = _dot_f32(x_ref[...], wr_ref[...])
    lane = lax.broadcasted_iota(I32, logits.shape, 1).astype(F32)
    neg = jnp.float32(-3.0e38)
    lg = jnp.where(lane < E, logits, neg)
    m1 = jnp.max(lg, axis=1, keepdims=True)
    i1 = jnp.min(jnp.where(lg == m1, lane, float(LANES)), axis=1, keepdims=True)
    lg2 = jnp.where(lane == i1, neg, lg)
    m2 = jnp.max(lg2, axis=1, keepdims=True)
    i2 = jnp.min(jnp.where(lg2 == m2, lane, float(LANES)), axis=1, keepdims=True)
    t = jnp.exp(m2 - m1)
    g1 = 1.0 / (1.0 + t)
    g2 = t * g1

    oh1 = lane == i1
    oh2 = lane == i2
    sel = jnp.where(oh1 | oh2, 1.0, 0.0)
    rr = lax.broadcasted_iota(I32, (tm, tm), 0)
    cc = lax.broadcasted_iota(I32, (tm, tm), 1)
    lower = jnp.where(rr > cc, 1.0, 0.0).astype(BF16)
    before = _dot(lower, sel.astype(BF16)) + carry_ref[...]
    r1 = jnp.sum(jnp.where(oh1, before, 0.0), axis=1, keepdims=True)
    r2 = jnp.sum(jnp.where(oh2, before, 0.0), axis=1, keepdims=True)
    carry_ref[...] += jnp.sum(sel, axis=0, keepdims=True)

    e_ref[:, 0:1] = i1.astype(I32)
    e_ref[:, 1:2] = i2.astype(I32)
    gate_ref[:, 0:1] = g1
    gate_ref[:, 1:2] = g2
    rank_ref[:, 0:1] = r1.astype(I32)
    rank_ref[:, 1:2] = r2.astype(I32)
    cnt_ref[...] = carry_ref[...].astype(I32)


def _router(x2, w_router):
    N, D = x2.shape
    E = w_router.shape[1]
    tm = min(ROUTER_TM, N)
    wr = jnp.pad(w_router, ((0, 0), (0, LANES - E)))
    return pl.pallas_call(
        functools.partial(_router_kernel, E=E),
        grid=(N // tm,),
        in_specs=[pl.BlockSpec((tm, D), lambda i: (i, 0)), _const_spec((D, LANES))],
        out_specs=[pl.BlockSpec((tm, TOP_K), lambda i: (i, 0)),
                   pl.BlockSpec((tm, TOP_K), lambda i: (i, 0)),
                   pl.BlockSpec((tm, TOP_K), lambda i: (i, 0)),
                   pl.BlockSpec((1, LANES), lambda i: (0, 0))],
        out_shape=[jax.ShapeDtypeStruct((N, TOP_K), I32),
                   jax.ShapeDtypeStruct((N, TOP_K), F32),
                   jax.ShapeDtypeStruct((N, TOP_K), I32),
                   jax.ShapeDtypeStruct((1, LANES), I32)],
        scratch_shapes=[pltpu.VMEM((1, LANES), F32)],
        compiler_params=_cparams(("arbitrary",)),
        name="moe_router",
    )(x2, wr)


def _row_copy(src, src_row, dst, dst_row, sem):
    return pltpu.make_async_copy(src.at[pl.ds(src_row, 1), :], dst.at[pl.ds(dst_row, 1), :], sem)


def _dispatch_kernel(dest_ref, x_ref, xs_in_hbm, xs_hbm, sem, *, tm):
    del xs_in_hbm

    def issue(r, c):
        for k in range(TOP_K):
            _row_copy(x_ref, r, xs_hbm, dest_ref[TOP_K * r + k], sem).start()
        return c

    lax.fori_loop(0, tm, issue, 0, unroll=DMA_UNROLL)

    def drain(r, c):
        for k in range(TOP_K):
            _row_copy(x_ref, r, xs_hbm, dest_ref[TOP_K * r + k], sem).wait()
        return c

    lax.fori_loop(0, tm, drain, 0, unroll=DMA_UNROLL)


def _dispatch(x2, dest_flat, n_slots):
    N, D = x2.shape
    tm = min(DMA_TM, N)
    xs0 = jnp.zeros((n_slots, D), F32)
    return pl.pallas_call(
        functools.partial(_dispatch_kernel, tm=tm),
        grid=(N // tm,),
        in_specs=[pl.BlockSpec((TOP_K * tm,), lambda i: (i,), memory_space=pltpu.SMEM),
                  pl.BlockSpec((tm, D), lambda i: (i, 0)),
                  pl.BlockSpec(memory_space=pl.ANY)],
        out_specs=pl.BlockSpec(memory_space=pl.ANY),
        out_shape=jax.ShapeDtypeStruct((n_slots, D), F32),
        scratch_shapes=[pltpu.SemaphoreType.DMA(())],
        input_output_aliases={2: 0},
        compiler_params=pltpu.CompilerParams(dimension_semantics=("arbitrary",), has_side_effects=True),
        name="moe_dispatch",
    )(dest_flat, x2, xs0)


def _combine_kernel(dest_ref, ys_hbm, x_ref, gate_ref, g_ref, b_ref, o_ref, buf_ref, sem, *, tm):
    def issue(r, c):
        for k in range(TOP_K):
            _row_copy(ys_hbm, dest_ref[TOP_K * r + k], buf_ref.at[k], r, sem).start()
        return c

    lax.fori_loop(0, tm, issue, 0, unroll=DMA_UNROLL)

    def drain(r, c):
        for k in range(TOP_K):
            _row_copy(ys_hbm, dest_ref[TOP_K * r + k], buf_ref.at[k], r, sem).wait()
        return c

    lax.fori_loop(0, tm, drain, 0, unroll=DMA_UNROLL)
    f = gate_ref[:, 0:1] * buf_ref[0] + gate_ref[:, 1:2] * buf_ref[1]
    z = DEEPNORM_ALPHA * x_ref[...] + f
    o_ref[...] = _layer_norm(z, g_ref[...], b_ref[...])


def _combine(ys, dest_flat, x2, gates, ln_g, ln_b):
    N, D = x2.shape
    tm = min(DMA_TM, N)
    return pl.pallas_call(
        functools.partial(_combine_kernel, tm=tm),
        grid=(N // tm,),
        in_specs=[pl.BlockSpec((TOP_K * tm,), lambda i: (i,), memory_space=pltpu.SMEM),
                  pl.BlockSpec(memory_space=pl.ANY),
                  pl.BlockSpec((tm, D), lambda i: (i, 0)),
                  pl.BlockSpec((tm, TOP_K), lambda i: (i, 0)),
                  _const_spec((1, D)), _const_spec((1, D))],
        out_specs=pl.BlockSpec((tm, D), lambda i: (i, 0)),
        out_shape=jax.ShapeDtypeStruct((N, D), F32),
        scratch_shapes=[pltpu.VMEM((TOP_K, tm, D), F32), pltpu.SemaphoreType.DMA(())],
        compiler_params=_cparams(("arbitrary",)),
        name="moe_combine",
    )(dest_flat, ys, x2, gates, ln_g, ln_b)


def _moe_layer(x2, w_router, w1, w3, w2, l, ln_g, ln_b, tm=MOE_TM):
    N, D = x2.shape
    E = w_router.shape[1]
    eidx, gates, rank, cnt = _router(x2, w_router)
    counts = cnt[0, :E]
    padded = ((counts + tm - 1) // tm) * tm
    pend = jnp.cumsum(padded)
    pstart = pend - padded
    n_assign = N * TOP_K
    nblk = -(-n_assign // tm) + E
    n_slots = nblk * tm
    dest = (pstart[eidx] + rank).reshape(-1).astype(I32)
    block_exp = jnp.minimum(
        jnp.searchsorted(pend, jnp.arange(nblk, dtype=I32) * tm, side="right"), E - 1).astype(I32)
    nb_used = (pend[-1] // tm).astype(I32).reshape(1)
    xs = _dispatch(x2, dest, n_slots)
    ys = _expert_ffn(xs, block_exp, nb_used, w1, w3, w2, l, tm)
    return _combine(ys, dest, x2, gates, ln_g, ln_b)


def _store_head_pairs(o_ref, r):
    for p in range(o_ref.shape[0]):
        o_ref[p] = r[:, p * LANES:(p + 1) * LANES].astype(o_ref.dtype)


def _kv_kernel(x_ref, wk_ref, wv_ref, k_ref, v_ref, *, pad_blocks):
    i = pl.program_id(1)

    @pl.when(i < pad_blocks)
    def _():
        k_ref[...] = jnp.zeros_like(k_ref)
        v_ref[...] = jnp.zeros_like(v_ref)

    @pl.when(i >= pad_blocks)
    def _():
        xb = x_ref[...].astype(BF16)
        _store_head_pairs(k_ref, _dot(xb, wk_ref[...]))
        _store_head_pairs(v_ref, _dot(xb, wv_ref[...]))


def _shared_kv(x, w_k, w_v, pad_rows):
    B, L, D = x.shape
    HD = w_k.shape[1]
    npair = HD // LANES
    tm = min(ROW_TILE, pad_rows)
    pad_blocks = pad_rows // tm
    spec_o = pl.BlockSpec((None, npair, tm, LANES), lambda b, i: (b, 0, i, 0))
    return pl.pallas_call(
        functools.partial(_kv_kernel, pad_blocks=pad_blocks),
        grid=(B, (L + pad_rows) // tm),
        in_specs=[pl.BlockSpec((None, tm, D), lambda b, i: (b, jnp.maximum(i - pad_blocks, 0), 0)),
                  _const_spec((D, HD)), _const_spec((D, HD))],
        out_specs=[spec_o, spec_o],
        out_shape=[jax.ShapeDtypeStruct((B, npair, L + pad_rows, LANES), BF16)] * 2,
        compiler_params=_cparams(("arbitrary", "arbitrary")),
        name="attn_kv",
    )(x, w_k, w_v)


def _qproj_kernel(x_ref, w_ref, o_ref, *, scale):
    _store_head_pairs(o_ref, _dot(x_ref[...].astype(BF16), w_ref[...]) * scale)


def _q_proj(x, w_q, l, scale):
    B, L, D = x.shape
    HD = w_q.shape[2]
    npair = HD // LANES
    tm = min(ROW_TILE, L)
    return pl.pallas_call(
        functools.partial(_qproj_kernel, scale=scale),
        grid=(B, L // tm),
        in_specs=[pl.BlockSpec((None, tm, D), lambda b, i: (b, i, 0)), _layer_spec((D, HD), l)],
        out_specs=pl.BlockSpec((None, npair, tm, LANES), lambda b, i: (b, 0, i, 0)),
        out_shape=jax.ShapeDtypeStruct((B, npair, L, LANES), BF16),
        compiler_params=_cparams(("arbitrary", "arbitrary")),
        name="attn_q",
    )(x, w_q)


def _attn_kernel(q_ref, k_ref, v_ref, bias_ref, o_ref, *, pad_rows):
    npg, tq, _ = q_ref.shape
    wk = tq + pad_rows
    dh = LANES // 2
    start = pl.multiple_of(pl.program_id(2) * tq, tq)
    lane = lax.broadcasted_iota(I32, (tq, LANES), 1)

    def compute(has_pad_keys):
        if has_pad_keys:
            real = lax.broadcasted_iota(I32, (tq, wk), 1) + start >= pad_rows
        for pp in range(npg):
            kwin = k_ref[pp, pl.ds(start, wk), :]
            vwin = v_ref[pp, pl.ds(start, wk), :]
            q2 = q_ref[pp]
            outs = []
            for h in range(2):
                in_head = (lane >= h * dh) & (lane < (h + 1) * dh)
                qh = jnp.where(in_head, q2, jnp.zeros_like(q2))
                s = lax.dot_general(qh, kwin, (((1,), (1,)), ((), ())), preferred_element_type=F32)
                s = s + bias_ref[pp, h]
                if has_pad_keys:
                    s = jnp.where(real, s, MASK_VALUE)
                m = jnp.max(s, axis=1, keepdims=True)
                p = jnp.exp2(s - m)
                l = jnp.sum(p, axis=1, keepdims=True)
                outs.append(_dot(p.astype(BF16), vwin) / l)
            o_ref[pp] = jnp.where(lane < dh, outs[0], outs[1]).astype(o_ref.dtype)

    @pl.when(start < pad_rows)
    def _():
        compute(True)

    @pl.when(start >= pad_rows)
    def _():
        compute(False)


def _attn_bias(rel_bias, tq, pad_rows):
    wk = tq + pad_rows
    nh = rel_bias.shape[0]
    n = tq - 1 + wk
    j = jnp.arange(n + 1)
    idx = jnp.clip(tq - 1 + pad_rows - j, -(CHUNK - 1), REL_CLIP) + (CHUNK - 1)
    v = rel_bias[:, idx].astype(F32) * LOG2E
    flat = jnp.tile(v, (1, tq))[:, :tq * n]
    toep = flat.reshape(nh, tq, n)[:, :, tq - 1:tq - 1 + wk]
    qc = jnp.arange(tq)[:, None] // CHUNK
    kc = jnp.arange(wk)[None, :] // CHUNK
    band = (kc >= qc) & (kc <= qc + LEFT_CHUNKS)
    bias = jnp.where(band[None], toep, MASK_VALUE)
    return bias.reshape(nh // 2, 2, tq, wk)


def _chunk_attention(q, kp, vp, bias, pad_rows):
    B, npair, L, _ = q.shape
    tq = bias.shape[2]
    wk = tq + pad_rows
    npg = min(ATTN_PAIRS, npair)
    lp = kp.shape[2]
    return pl.pallas_call(
        functools.partial(_attn_kernel, pad_rows=pad_rows),
        grid=(npair // npg, B, L // tq),
        in_specs=[pl.BlockSpec((None, npg, tq, LANES), lambda g, b, i: (b, g, i, 0)),
                  pl.BlockSpec((None, npg, lp, LANES), lambda g, b, i: (b, g, 0, 0)),
                  pl.BlockSpec((None, npg, lp, LANES), lambda g, b, i: (b, g, 0, 0)),
                  pl.BlockSpec((npg, 2, tq, wk), lambda g, b, i: (g, 0, 0, 0))],
        out_specs=pl.BlockSpec((None, npg, tq, LANES), lambda g, b, i: (b, g, i, 0)),
        out_shape=jax.ShapeDtypeStruct((B, npair, L, LANES), BF16),
        compiler_params=_cparams(("arbitrary", "arbitrary", "arbitrary")),
        name="attn_core",
    )(q, kp, vp, bias)


def _oproj_kernel(a_ref, w_ref, x_ref, g_ref, b_ref, o_ref):
    a = jnp.concatenate([a_ref[p] for p in range(a_ref.shape[0])], axis=1)
    z = DEEPNORM_ALPHA * x_ref[...] + _dot(a, w_ref[...])
    o_ref[...] = _layer_norm(z, g_ref[...], b_ref[...])


def _o_proj(a, w_o, l, x, ln_g, ln_b):
    B, L, D = x.shape
    npair = a.shape[1]
    HD = npair * LANES
    tm = min(ROW_TILE, L)
    return pl.pallas_call(
        _oproj_kernel,
        grid=(B, L // tm),
        in_specs=[pl.BlockSpec((None, npair, tm, LANES), lambda b, i: (b, 0, i, 0)),
                  _layer_spec((HD, D), l),
                  pl.BlockSpec((None, tm, D), lambda b, i: (b, i, 0)),
                  _const_spec((1, D)), _const_spec((1, D))],
        out_specs=pl.BlockSpec((None, tm, D), lambda b, i: (b, i, 0)),
        out_shape=jax.ShapeDtypeStruct((B, L, D), F32),
        compiler_params=_cparams(("arbitrary", "arbitrary")),
        name="attn_o",
    )(a, w_o, x, ln_g, ln_b)


def _attn_layer(x, kp, vp, w_q, rel_bias, w_o, l, ln_g, ln_b, pad_rows):
    B, L, D = x.shape
    HD = w_q.shape[2]
    dh = HD // rel_bias.shape[0]
    tq = min(ATTN_TQ, L)
    q = _q_proj(x, w_q, l, dh ** -0.5 * LOG2E)
    bias = _attn_bias(rel_bias, tq, pad_rows)
    o = _chunk_attention(q, kp, vp, bias, pad_rows)
    return _o_proj(o, w_o, l, x, ln_g, ln_b)


def _pad_ff(w1, w3, w2, tf):
    F = w1.shape[2]
    Fp = -(-F // tf) * tf
    w1 = jnp.pad(w1, ((0, 0), (0, 0), (0, Fp - F))).astype(BF16)
    w3 = jnp.pad(w3, ((0, 0), (0, 0), (0, Fp - F))).astype(BF16)
    w2 = jnp.pad(w2, ((0, 0), (0, Fp - F), (0, 0))).astype(BF16)
    return w1, w3, w2


def kernel(x, ssm_w_in, ssm_log_step, ssm_lam_re, ssm_lam_im, ssm_b_re, ssm_b_im, ssm_c_re, ssm_c_im,
           ssm_d, ssm_w_glu, attn_w_k, attn_w_v, attn_w_q, attn_rel_bias, attn_w_o,
           ffn_w1, ffn_w3, ffn_w2, moe_router, moe_w1, moe_w3, moe_w2, ln_g, ln_b):
    B, L, D = x.shape
    n_a = ssm_w_in.shape[0]
    depth = ln_g.shape[0]
    T = min(S5_T, L)
    pad_rows = LEFT_CHUNKS * CHUNK
    w_in_t = jnp.swapaxes(ssm_w_in, 1, 2).astype(BF16)
    w_glu = ssm_w_glu.astype(BF16)
    w_q, w_o = attn_w_q.astype(BF16), attn_w_o.astype(BF16)
    f1, f3, f2 = _pad_ff(ffn_w1, ffn_w3, ffn_w2, FF_TILE)
    m1, m3, m2 = moe_w1.astype(BF16), moe_w3.astype(BF16), moe_w2.astype(BF16)
    kp = vp = None
    for l in range(depth):
        g0, b0 = ln_g[l, 0].reshape(1, D), ln_b[l, 0].reshape(1, D)
        g1, b1 = ln_g[l, 1].reshape(1, D), ln_b[l, 1].reshape(1, D)
        if l < n_a:
            ops = _s5_params(ssm_log_step[l], ssm_lam_re[l], ssm_lam_im[l], ssm_b_re[l], ssm_b_im[l],
                             ssm_c_re[l], ssm_c_im[l], ssm_d[l], T)
            x = _s5_layer(x, w_in_t, w_glu, l, ops, g0, b0, T)
        else:
            x = _attn_layer(x, kp, vp, w_q, attn_rel_bias[l - n_a], w_o, l - n_a, g0, b0, pad_rows)
        x2 = x.reshape(B * L, D)
        if l % 2 == 0:
            x2 = _ffn_layer(x2, f1, f3, f2, l // 2, g1, b1)
        else:
            x2 = _moe_layer(x2, moe_router[l // 2], m1, m3, m2, l // 2, g1, b1)
        x = x2.reshape(B, L, D)
        if l == n_a - 1:
            kp, vp = _shared_kv(x, attn_w_k.astype(BF16), attn_w_v.astype(BF16), pad_rows)
    return x
```

```python
import functools
import math

import jax
import jax.numpy as jnp
from jax import lax
from jax.experimental import pallas as pl
from jax.experimental.pallas import tpu as pltpu

F32 = jnp.float32
BF16 = jnp.bfloat16
I32 = jnp.int32

DEPTH = 4
CHUNK = 64
SSM_GROUP = 16
M_SHIFT = SSM_GROUP.bit_length() - 1
SSM_STATE = 64
ATTN_HEADS = 32
LEFT_CHUNKS = 8
REL_CLIP = 128
N_EXPERTS = 8
TOP_K = 2
DEEPNORM_ALPHA = (2.0 * DEPTH) ** 0.25
LN_EPS = 1e-5
MASK_VALUE = -1e30
LOG2E = math.log2(math.e)

LANES = 128
VMEM_LIMIT = 56 * 1024 * 1024

S5_T = 64
ROW_TILE = 512
FFN_TM = 512
FF_TILE = 512
MOE_FF_TILE = 1024
ATTN_TQ = 256
ATTN_PAIRS = 4
MOE_TM = 512
ROUTER_TM = 512
DMA_TM = 256
DMA_UNROLL = 16


def _cparams(sem, vmem=VMEM_LIMIT):
    return pltpu.CompilerParams(dimension_semantics=sem, vmem_limit_bytes=vmem)


def _dot(a, b):
    return jnp.dot(a, b, preferred_element_type=F32)


def _split3(a):
    hi = a.astype(BF16)
    r = a - hi.astype(F32)
    mid = r.astype(BF16)
    lo = (r - mid.astype(F32)).astype(BF16)
    return hi, mid, lo


def _dot_f32(a, b):
    ah, am, al = _split3(a)
    bh, bm, bl = _split3(b)
    small = _dot(am, bm) + _dot(ah, bl) + _dot(al, bh)
    return _dot(ah, bh) + (_dot(ah, bm) + _dot(am, bh) + small)


def _copy_dot(a, b, split_left):
    if split_left:
        return sum(_dot(piece, b) for piece in _split3(a))
    return sum(_dot(a, piece) for piece in _split3(b))


def _layer_norm(z, g, b):
    mu = jnp.mean(z, axis=-1, keepdims=True)
    zc = z - mu
    var = jnp.mean(zc * zc, axis=-1, keepdims=True)
    return zc * lax.rsqrt(var + LN_EPS) * g + b


def _sigmoid(x):
    return 1.0 / (1.0 + jnp.exp(-x))


def _gelu_tanh(x):
    c = math.sqrt(2.0 / math.pi)
    return 0.5 * x * (1.0 + jnp.tanh(c * (x + 0.044715 * (x * x * x))))


def _const_spec(shape):
    nd = len(shape)
    return pl.BlockSpec(shape, lambda *_: (0,) * nd, pipeline_mode=pl.Buffered(1))


def _layer_spec(shape, l):
    nd = len(shape)
    return pl.BlockSpec((None,) + tuple(shape), lambda *_: (l,) + (0,) * nd, pipeline_mode=pl.Buffered(1))


def _cpow(er, ei, n, nbits):
    pr = jnp.ones(n.shape, F32)
    pi = jnp.zeros(n.shape, F32)
    br, bi = er, ei
    for k in range(nbits):
        bit = ((n >> k) & 1) == 1
        fr = jnp.where(bit, br, 1.0)
        fi = jnp.where(bit, bi, 0.0)
        pr, pi = pr * fr - pi * fi, pr * fi + pi * fr
        br, bi = br * br - bi * bi, 2.0 * br * bi
    return pr, pi


def _s5_param_kernel(ls_ref, lamc_ref, lamr_ref, btr_ref, bti_ref, ccat_ref, a1_ref, a2_ref, d_ref,
                     at_ref, wt_ref, vt_ref, lt_ref, *, T):
    M = SSM_GROUP
    P = lamc_ref.shape[0]
    TM = T * M
    nb = max(1, int(T).bit_length())
    delta = jnp.exp(ls_ref[...])

    lr = lamc_ref[:, 0:1]
    li = lamc_ref[:, 1:2]
    mag = jnp.exp(lr * delta)
    er = mag * jnp.cos(li * delta)
    ei = mag * jnp.sin(li * delta)
    den = lr * lr + li * li
    nr = er - 1.0
    cfr = (nr * lr + ei * li) / den
    cfi = (ei * lr - nr * li) / den
    jrev = (T - 1) - lax.broadcasted_iota(I32, (P, T), 1)
    pr_t, pi_t = _cpow(er, ei, jrev, nb)
    slot = lax.broadcasted_iota(I32, (T, TM), 1) >> M_SHIFT
    spread = jnp.where(slot == lax.broadcasted_iota(I32, (T, TM), 0), 1.0, 0.0).astype(BF16)
    pr = _copy_dot(pr_t, spread, split_left=True)
    pi = _copy_dot(pi_t, spread, split_left=True)
    btr = btr_ref[...]
    bti = bti_ref[...]
    bbr = cfr * btr - cfi * bti
    bbi = cfr * bti + cfi * btr
    lbr = pr * bbr - pi * bbi
    lbi = pr * bbi + pi * bbr
    wt = jnp.concatenate([lbr, lbi], axis=0)
    wt_ref[...] = wt.astype(BF16)

    krev = _dot_f32(ccat_ref[...], wt)
    row = lax.broadcasted_iota(I32, (M, TM), 0)
    lane2 = lax.broadcasted_iota(I32, (M, TM), 1)
    krev = krev + jnp.where(lane2 == (T - 1) * M + row, d_ref[...], 0.0)
    kext = jnp.concatenate([krev, jnp.zeros_like(krev)], axis=1)
    for t in range(T):
        off = (T - 1 - t) * M
        win = kext if off == 0 else pltpu.roll(kext, 2 * TM - off, axis=1)
        at_ref[t * M:(t + 1) * M, :] = win[:, :TM].astype(BF16)

    lrr = lamr_ref[0:1, :]
    lir = lamr_ref[1:2, :]
    magr = jnp.exp(lrr * delta)
    err = magr * jnp.cos(lir * delta)
    eir = magr * jnp.sin(lir * delta)
    qr_t, qi_t = _cpow(err, eir, lax.broadcasted_iota(I32, (T, 2 * P), 0) + 1, nb)
    rslot = lax.broadcasted_iota(I32, (TM, T), 0) >> M_SHIFT
    rspread = jnp.where(rslot == lax.broadcasted_iota(I32, (TM, T), 1), 1.0, 0.0).astype(BF16)
    qr = _copy_dot(rspread, qr_t, split_left=False)
    qi = _copy_dot(rspread, qi_t, split_left=False)
    vt_ref[...] = (a1_ref[...] * qr + a2_ref[...] * qi).astype(BF16)

    tr, ti = er, ei
    ar = jnp.ones_like(er)
    ai = jnp.zeros_like(ei)
    n = T
    while n:
        if n & 1:
            ar, ai = ar * tr - ai * ti, ar * ti + ai * tr
        tr, ti = tr * tr - ti * ti, 2.0 * tr * ti
        n >>= 1
    lt_ref[...] = jnp.concatenate([ar, ai], axis=0)


def _s5_params(log_step, lam_re, lam_im, b_re, b_im, c_re, c_im, d_skip, T):
    G, P = lam_re.shape
    M = SSM_GROUP
    TM = T * M
    ls = log_step.reshape(G, 1, 1)
    lamc = jnp.stack([lam_re, lam_im], axis=-1)
    lamr = jnp.stack([jnp.tile(lam_re, (1, 2)), jnp.tile(lam_im, (1, 2))], 1)
    btr = jnp.tile(b_re, (1, 1, T))
    bti = jnp.tile(b_im, (1, 1, T))
    ccat = jnp.concatenate([c_re, -c_im], axis=-1)
    ctr = jnp.tile(c_re, (1, T, 1))
    cti = jnp.tile(c_im, (1, T, 1))
    a1 = jnp.concatenate([ctr, -cti], axis=-1)
    a2 = jnp.concatenate([-cti, -ctr], axis=-1)
    dcol = d_skip.reshape(G, M, 1)

    def gspec(*shape):
        nd = len(shape)
        return pl.BlockSpec((None,) + shape, lambda g: (g,) + (0,) * nd)

    return pl.pallas_call(
        functools.partial(_s5_param_kernel, T=T),
        grid=(G,),
        in_specs=[gspec(1, 1), gspec(P, 2), gspec(2, 2 * P), gspec(P, TM), gspec(P, TM),
                  gspec(M, 2 * P), gspec(TM, 2 * P), gspec(TM, 2 * P), gspec(M, 1)],
        out_specs=[gspec(TM, TM), gspec(2 * P, TM), gspec(TM, 2 * P), gspec(2 * P, 1)],
        out_shape=[jax.ShapeDtypeStruct((G, TM, TM), BF16),
                   jax.ShapeDtypeStruct((G, 2 * P, TM), BF16),
                   jax.ShapeDtypeStruct((G, TM, 2 * P), BF16),
                   jax.ShapeDtypeStruct((G, 2 * P, 1), F32)],
        compiler_params=_cparams(("arbitrary",)),
        name="s5_params",
    )(ls, lamc, lamr, btr, bti, ccat, a1, a2, dcol)


def _s5_uproj_kernel(x_ref, w_ref, o_ref):
    xs = x_ref[...].astype(BF16)
    ut = lax.dot_general(w_ref[...], xs, (((1,), (1,)), ((), ())), preferred_element_type=F32)
    o_ref[...] = ut.astype(BF16).reshape(o_ref.shape)


def _s5_scan_kernel(ut_ref, at_ref, wt_ref, vt_ref, lt_ref, y_ref, *, NC):
    T, M, BC = ut_ref.shape
    P = lt_ref.shape[0] // 2
    u = ut_ref[...].reshape(T * M, BC)
    z = _dot(wt_ref[...], u)
    y_local = _dot(at_ref[...], u)
    hr, hi = z[:P], z[P:]
    dr = lt_ref[:P, :]
    di = lt_ref[P:, :]
    cpos = lax.broadcasted_iota(I32, (P, BC), 1) & (NC - 1)
    sh = 1
    while sh < NC:
        m = cpos >= sh
        sr = jnp.where(m, pltpu.roll(hr, sh, axis=1), 0.0)
        si = jnp.where(m, pltpu.roll(hi, sh, axis=1), 0.0)
        hr, hi = hr + (dr * sr - di * si), hi + (dr * si + di * sr)
        dr, di = dr * dr - di * di, 2.0 * dr * di
        sh *= 2
    m1 = cpos >= 1
    pr = jnp.where(m1, pltpu.roll(hr, 1, axis=1), 0.0)
    pi = jnp.where(m1, pltpu.roll(hi, 1, axis=1), 0.0)
    hp = jnp.concatenate([pr, pi], axis=0).astype(BF16)
    y = y_local + _dot(vt_ref[...], hp)
    y_ref[...] = _gelu_tanh(y).astype(BF16).reshape(y_ref.shape)


def _s5_out_kernel(y_ref, w_ref, x_ref, g_ref, b_ref, o_ref):
    G, M, bc = y_ref.shape
    d = x_ref.shape[1]
    yt = y_ref[...].reshape(G * M, bc)
    o = lax.dot_general(yt, w_ref[...], (((0,), (0,)), ((), ())), preferred_element_type=F32)
    mix = o[:, :d] * _sigmoid(o[:, d:])
    z = DEEPNORM_ALPHA * x_ref[...] + mix
    o_ref[...] = _layer_norm(z, g_ref[...], b_ref[...])


def _s5_layer(x, w_in_t, w_glu, l, ops, ln_g, ln_b, T):
    B, L, D = x.shape
    H = w_in_t.shape[1]
    M = SSM_GROUP
    G = H // M
    NC = L // T
    BC = B * NC
    at, wt, vt, lt = ops
    P2 = wt.shape[1]
    x3 = x.reshape(BC, T * D)

    ut = pl.pallas_call(
        _s5_uproj_kernel,
        grid=(T,),
        in_specs=[pl.BlockSpec((BC, D), lambda s: (0, s)), _layer_spec((H, D), l)],
        out_specs=pl.BlockSpec((G, None, M, BC), lambda s: (0, s, 0, 0)),
        out_shape=jax.ShapeDtypeStruct((G, T, M, BC), BF16),
        compiler_params=_cparams(("arbitrary",)),
        name="s5_uproj",
    )(x3, w_in_t)

    ys = pl.pallas_call(
        functools.partial(_s5_scan_kernel, NC=NC),
        grid=(G,),
        in_specs=[pl.BlockSpec((None, T, M, BC), lambda g: (g, 0, 0, 0)),
                  pl.BlockSpec((None, T * M, T * M), lambda g: (g, 0, 0)),
                  pl.BlockSpec((None, P2, T * M), lambda g: (g, 0, 0)),
                  pl.BlockSpec((None, T * M, P2), lambda g: (g, 0, 0)),
                  pl.BlockSpec((None, P2, 1), lambda g: (g, 0, 0))],
        out_specs=pl.BlockSpec((None, T, M, BC), lambda g: (g, 0, 0, 0)),
        out_shape=jax.ShapeDtypeStruct((G, T, M, BC), BF16),
        compiler_params=_cparams(("arbitrary",)),
        name="s5_scan",
    )(ut, at, wt, vt, lt)

    nsplit = 2 if (BC // 2) % LANES == 0 else 1
    bct = BC // nsplit
    out = pl.pallas_call(
        _s5_out_kernel,
        grid=(T, nsplit),
        in_specs=[pl.BlockSpec((G, None, M, bct), lambda s, h: (0, s, 0, h)),
                  _layer_spec((H, 2 * D), l),
                  pl.BlockSpec((bct, D), lambda s, h: (h, s)),
                  _const_spec((1, D)), _const_spec((1, D))],
        out_specs=pl.BlockSpec((bct, D), lambda s, h: (h, s)),
        out_shape=jax.ShapeDtypeStruct((BC, T * D), F32),
        compiler_params=_cparams(("arbitrary", "arbitrary")),
        name="s5_out",
    )(ys, w_glu, x3, ln_g, ln_b)
    return out.reshape(B, L, D)


def _swiglu_step(xb_ref, acc_ref, w1_ref, w3_ref, w2_ref):
    xb = xb_ref[...]
    h1 = _dot(xb, w1_ref[...])
    h3 = _dot(xb, w3_ref[...])
    h = (h1 * _sigmoid(h1) * h3).astype(BF16)
    acc_ref[...] += _dot(h, w2_ref[...])


def _ffn_kernel(x_ref, w1_ref, w3_ref, w2_ref, g_ref, b_ref, o_ref, xb_ref):
    j = pl.program_id(1)

    @pl.when(j == 0)
    def _():
        xb_ref[...] = x_ref[...].astype(BF16)
        o_ref[...] = jnp.zeros_like(o_ref)

    _swiglu_step(xb_ref, o_ref, w1_ref, w3_ref, w2_ref)

    @pl.when(j == pl.num_programs(1) - 1)
    def _():
        z = DEEPNORM_ALPHA * x_ref[...] + o_ref[...]
        o_ref[...] = _layer_norm(z, g_ref[...], b_ref[...])


def _ffn_layer(x2, w1, w3, w2, l, ln_g, ln_b, tm=None, tf=FF_TILE):
    N, D = x2.shape
    Fp = w1.shape[2]
    tm = min(FFN_TM if tm is None else tm, N)
    tf = min(tf, Fp)
    return pl.pallas_call(
        _ffn_kernel,
        grid=(N // tm, Fp // tf),
        in_specs=[pl.BlockSpec((tm, D), lambda i, j: (i, 0)),
                  pl.BlockSpec((None, D, tf), lambda i, j: (l, 0, j)),
                  pl.BlockSpec((None, D, tf), lambda i, j: (l, 0, j)),
                  pl.BlockSpec((None, tf, D), lambda i, j: (l, j, 0)),
                  _const_spec((1, D)), _const_spec((1, D))],
        out_specs=pl.BlockSpec((tm, D), lambda i, j: (i, 0)),
        out_shape=jax.ShapeDtypeStruct((N, D), F32),
        scratch_shapes=[pltpu.VMEM((tm, D), BF16)],
        compiler_params=_cparams(("arbitrary", "arbitrary")),
        name="ffn_dense",
    )(x2, w1, w3, w2, ln_g, ln_b)


def _expert_kernel(be_ref, nb_ref, x_ref, w1_ref, w3_ref, w2_ref, y_ref, xb_ref):
    i = pl.program_id(0)
    j = pl.program_id(1)

    @pl.when(j == 0)
    def _():
        y_ref[...] = jnp.zeros_like(y_ref)

    @pl.when(i < nb_ref[0])
    def _():
        @pl.when(j == 0)
        def _():
            xb_ref[...] = x_ref[...].astype(BF16)

        _swiglu_step(xb_ref, y_ref, w1_ref, w3_ref, w2_ref)


def _expert_ffn(xs, block_exp, nb_used, w1, w3, w2, l, tm, tf=None):
    Pm, D = xs.shape
    _, E, _, F = w1.shape
    tf = MOE_FF_TILE if tf is None else tf
    tf = min(tf, F)
    nblk = Pm // tm
    nj = F // tf

    def row_map(i, j, be, nb):
        return (jnp.minimum(i, nb[0] - 1), 0)

    def jj(i, j, nb):
        return jnp.where(i < nb[0], j, nj - 1)

    def w13_map(i, j, be, nb):
        return (l, be[jnp.minimum(i, nb[0] - 1)], 0, jj(i, j, nb))

    def w2_map(i, j, be, nb):
        return (l, be[jnp.minimum(i, nb[0] - 1)], jj(i, j, nb), 0)

    grid_spec = pltpu.PrefetchScalarGridSpec(
        num_scalar_prefetch=2,
        grid=(nblk, nj),
        in_specs=[pl.BlockSpec((tm, D), row_map),
                  pl.BlockSpec((None, None, D, tf), w13_map),
                  pl.BlockSpec((None, None, D, tf), w13_map),
                  pl.BlockSpec((None, None, tf, D), w2_map)],
        out_specs=pl.BlockSpec((tm, D), lambda i, j, be, nb: (i, 0)),
        scratch_shapes=[pltpu.VMEM((tm, D), BF16)],
    )
    return pl.pallas_call(
        _expert_kernel,
        grid_spec=grid_spec,
        out_shape=jax.ShapeDtypeStruct((Pm, D), F32),
        compiler_params=_cparams(("arbitrary", "arbitrary")),
        name="moe_experts",
    )(block_exp, nb_used, xs, w1, w3, w2)


def _router_kernel(x_ref, wr_ref, e_ref, gate_ref, rank_ref, cnt_ref, carry_ref, *, E):
    i = pl.program_id(0)

    @pl.when(i == 0)
    def _():
        carry_ref[...] = jnp.zeros_like(carry_ref)

    tm = x_ref.shape[0]
    logits = _dot_f32(x_ref[...], wr_ref[...])
    lane = lax.broadcasted_iota(I32, logits.shape, 1).astype(F32)
    neg = jnp.float32(-3.0e38)
    lg = jnp.where(lane < E, logits, neg)
    m1 = jnp.max(lg, axis=1, keepdims=True)
    i1 = jnp.min(jnp.where(lg == m1, lane, float(LANES)), axis=1, keepdims=True)
    lg2 = jnp.where(lane == i1, neg, lg)
    m2 = jnp.max(lg2, axis=1, keepdims=True)
    i2 = jnp.min(jnp.where(lg2 == m2, lane, float(LANES)), axis=1, keepdims=True)
    t = jnp.exp(m2 - m1)
    g1 = 1.0 / (1.0 + t)
    g2 = t * g1

    oh1 = lane == i1
    oh2 = lane == i2
    sel = jnp.where(oh1 | oh2, 1.0, 0.0)
    rr = lax.broadcasted_iota(I32, (tm, tm), 0)
    cc = lax.broadcasted_iota(I32, (tm, tm), 1)
    lower = jnp.where(rr > cc, 1.0, 0.0).astype(BF16)
    before = _dot(lower, sel.astype(BF16)) + carry_ref[...]
    r1 = jnp.sum(jnp.where(oh1, before, 0.0), axis=1, keepdims=True)
    r2 = jnp.sum(jnp.where(oh2, before, 0.0), axis=1, keepdims=True)
    carry_ref[...] += jnp.sum(sel, axis=0, keepdims=True)

    e_ref[:, 0:1] = i1.astype(I32)
    e_ref[:, 1:2] = i2.astype(I32)
    gate_ref[:, 0:1] = g1
    gate_ref[:, 1:2] = g2
    rank_ref[:, 0:1] = r1.astype(I32)
    rank_ref[:, 1:2] = r2.astype(I32)
    cnt_ref[...] = carry_ref[...].astype(I32)


def _router(x2, w_router):
    N, D = x2.shape
    E = w_router.shape[1]
    tm = min(ROUTER_TM, N)
    wr = jnp.pad(w_router, ((0, 0), (0, LANES - E)))
    return pl.pallas_call(
        functools.partial(_router_kernel, E=E),
        grid=(N // tm,),
        in_specs=[pl.BlockSpec((tm, D), lambda i: (i, 0)), _const_spec((D, LANES))],
        out_specs=[pl.BlockSpec((tm, TOP_K), lambda i: (i, 0)),
                   pl.BlockSpec((tm, TOP_K), lambda i: (i, 0)),
                   pl.BlockSpec((tm, TOP_K), lambda i: (i, 0)),
                   pl.BlockSpec((1, LANES), lambda i: (0, 0))],
        out_shape=[jax.ShapeDtypeStruct((N, TOP_K), I32),
                   jax.ShapeDtypeStruct((N, TOP_K), F32),
                   jax.ShapeDtypeStruct((N, TOP_K), I32),
                   jax.ShapeDtypeStruct((1, LANES), I32)],
        scratch_shapes=[pltpu.VMEM((1, LANES), F32)],
        compiler_params=_cparams(("arbitrary",)),
        name="moe_router",
    )(x2, wr)


def _row_copy(src, src_row, dst, dst_row, sem):
    return pltpu.make_async_copy(src.at[pl.ds(src_row, 1), :], dst.at[pl.ds(dst_row, 1), :], sem)


def _dispatch_kernel(pend_ref, nbu_ref, dest_ref, x_ref, xs_hbm, zero_ref, sem, zsem, *, tm, block_rows, n_blocks):
    parts = block_rows // tm

    def zero_block(row0):
        row0 = pl.multiple_of(row0, tm)
        return [pltpu.make_async_copy(zero_ref, xs_hbm.at[pl.ds(row0 + k * tm, tm), :], zsem) for k in range(parts)]

    @pl.when(pl.program_id(0) == 0)
    def _():
        zero_ref[...] = jnp.zeros_like(zero_ref)

        def fill(row0):
            copies = zero_block(row0)
            for cp in copies:
                cp.start()
            for cp in copies:
                cp.wait()

        for e in range(pend_ref.shape[0]):
            fill(jnp.maximum(pend_ref[e] - block_rows, 0))

        def unused(j, c):
            fill(j * block_rows)
            return c

        lax.fori_loop(nbu_ref[0], n_blocks, unused, 0)

    def issue(r, c):
        for k in range(TOP_K):
            _row_copy(x_ref, r, xs_hbm, dest_ref[TOP_K * r + k], sem).start()
        return c

    lax.fori_loop(0, tm, issue, 0, unroll=DMA_UNROLL)

    def drain(r, c):
        for k in range(TOP_K):
            _row_copy(x_ref, r, xs_hbm, dest_ref[TOP_K * r + k], sem).wait()
        return c

    lax.fori_loop(0, tm, drain, 0, unroll=DMA_UNROLL)


def _dispatch(x2, dest_flat, pend, nb_used, block_rows, n_blocks):
    N, D = x2.shape
    tm = min(DMA_TM, N, block_rows)
    grid_spec = pltpu.PrefetchScalarGridSpec(
        num_scalar_prefetch=2,
        grid=(N // tm,),
        in_specs=[pl.BlockSpec((TOP_K * tm,), lambda i, pe, nb: (i,), memory_space=pltpu.SMEM),
                  pl.BlockSpec((tm, D), lambda i, pe, nb: (i, 0))],
        out_specs=pl.BlockSpec(memory_space=pl.ANY),
        scratch_shapes=[pltpu.VMEM((tm, D), F32), pltpu.SemaphoreType.DMA(()), pltpu.SemaphoreType.DMA(())],
    )
    return pl.pallas_call(
        functools.partial(_dispatch_kernel, tm=tm, block_rows=block_rows, n_blocks=n_blocks),
        grid_spec=grid_spec,
        out_shape=jax.ShapeDtypeStruct((n_blocks * block_rows, D), F32),
        compiler_params=_cparams(("arbitrary",)),
        name="moe_dispatch",
    )(pend, nb_used, dest_flat, x2)


def _combine_kernel(dest_ref, ys_hbm, x_ref, gate_ref, g_ref, b_ref, o_ref, buf_ref, sem, *, tm):
    def issue(r, c):
        for k in range(TOP_K):
            _row_copy(ys_hbm, dest_ref[TOP_K * r + k], buf_ref.at[k], r, sem).start()
        return c

    lax.fori_loop(0, tm, issue, 0, unroll=DMA_UNROLL)

    def drain(r, c):
        for k in range(TOP_K):
            _row_copy(ys_hbm, dest_ref[TOP_K * r + k], buf_ref.at[k], r, sem).wait()
        return c

    lax.fori_loop(0, tm, drain, 0, unroll=DMA_UNROLL)
    f = gate_ref[:, 0:1] * buf_ref[0] + gate_ref[:, 1:2] * buf_ref[1]
    z = DEEPNORM_ALPHA * x_ref[...] + f
    o_ref[...] = _layer_norm(z, g_ref[...], b_ref[...])


def _combine(ys, dest_flat, x2, gates, ln_g, ln_b):
    N, D = x2.shape
    tm = min(DMA_TM, N)
    return pl.pallas_call(
        functools.partial(_combine_kernel, tm=tm),
        grid=(N // tm,),
        in_specs=[pl.BlockSpec((TOP_K * tm,), lambda i: (i,), memory_space=pltpu.SMEM),
                  pl.BlockSpec(memory_space=pl.ANY),
                  pl.BlockSpec((tm, D), lambda i: (i, 0)),
                  pl.BlockSpec((tm, TOP_K), lambda i: (i, 0)),
                  _const_spec((1, D)), _const_spec((1, D))],
        out_specs=pl.BlockSpec((tm, D), lambda i: (i, 0)),
        out_shape=jax.ShapeDtypeStruct((N, D), F32),
        scratch_shapes=[pltpu.VMEM((TOP_K, tm, D), F32), pltpu.SemaphoreType.DMA(())],
        compiler_params=_cparams(("arbitrary",)),
        name="moe_combine",
    )(dest_flat, ys, x2, gates, ln_g, ln_b)


def _moe_layer(x2, w_router, w1, w3, w2, l, ln_g, ln_b, tm=MOE_TM):
    N, D = x2.shape
    E = w_router.shape[1]
    eidx, gates, rank, cnt = _router(x2, w_router)
    counts = cnt[0, :E]
    padded = ((counts + tm - 1) // tm) * tm
    pend = jnp.cumsum(padded)
    pstart = pend - padded
    n_assign = N * TOP_K
    nblk = -(-n_assign // tm) + E
    dest = (pstart[eidx] + rank).reshape(-1).astype(I32)
    block_exp = jnp.minimum(
        jnp.searchsorted(pend, jnp.arange(nblk, dtype=I32) * tm, side="right"), E - 1).astype(I32)
    nb_used = (pend[-1] // tm).astype(I32).reshape(1)
    xs = _dispatch(x2, dest, pend.astype(I32), nb_used, tm, nblk)
    ys = _expert_ffn(xs, block_exp, nb_used, w1, w3, w2, l, tm)
    return _combine(ys, dest, x2, gates, ln_g, ln_b)


def _store_head_pairs(o_ref, r):
    for p in range(o_ref.shape[0]):
        o_ref[p] = r[:, p * LANES:(p + 1) * LANES].astype(o_ref.dtype)


def _kv_kernel(x_ref, wk_ref, wv_ref, k_ref, v_ref, *, pad_blocks):
    i = pl.program_id(1)

    @pl.when(i < pad_blocks)
    def _():
        k_ref[...] = jnp.zeros_like(k_ref)
        v_ref[...] = jnp.zeros_like(v_ref)

    @pl.when(i >= pad_blocks)
    def _():
        xb = x_ref[...].astype(BF16)
        _store_head_pairs(k_ref, _dot(xb, wk_ref[...]))
        _store_head_pairs(v_ref, _dot(xb, wv_ref[...]))


def _shared_kv(x, w_k, w_v, pad_rows):
    B, L, D = x.shape
    HD = w_k.shape[1]
    npair = HD // LANES
    tm = min(ROW_TILE, pad_rows)
    pad_blocks = pad_rows // tm
    spec_o = pl.BlockSpec((None, npair, tm, LANES), lambda b, i: (b, 0, i, 0))
    return pl.pallas_call(
        functools.partial(_kv_kernel, pad_blocks=pad_blocks),
        grid=(B, (L + pad_rows) // tm),
        in_specs=[pl.BlockSpec((None, tm, D), lambda b, i: (b, jnp.maximum(i - pad_blocks, 0), 0)),
                  _const_spec((D, HD)), _const_spec((D, HD))],
        out_specs=[spec_o, spec_o],
        out_shape=[jax.ShapeDtypeStruct((B, npair, L + pad_rows, LANES), BF16)] * 2,
        compiler_params=_cparams(("arbitrary", "arbitrary")),
        name="attn_kv",
    )(x, w_k, w_v)


def _qproj_kernel(x_ref, w_ref, o_ref, *, scale):
    _store_head_pairs(o_ref, _dot(x_ref[...].astype(BF16), w_ref[...]) * scale)


def _q_proj(x, w_q, l, scale):
    B, L, D = x.shape
    HD = w_q.shape[2]
    npair = HD // LANES
    tm = min(ROW_TILE, L)
    return pl.pallas_call(
        functools.partial(_qproj_kernel, scale=scale),
        grid=(B, L // tm),
        in_specs=[pl.BlockSpec((None, tm, D), lambda b, i: (b, i, 0)), _layer_spec((D, HD), l)],
        out_specs=pl.BlockSpec((None, npair, tm, LANES), lambda b, i: (b, 0, i, 0)),
        out_shape=jax.ShapeDtypeStruct((B, npair, L, LANES), BF16),
        compiler_params=_cparams(("arbitrary", "arbitrary")),
        name="attn_q",
    )(x, w_q)


def _attn_kernel(q_ref, k_ref, v_ref, bias_ref, o_ref, *, pad_rows):
    npg, tq, _ = q_ref.shape
    wk = tq + pad_rows
    dh = LANES // 2
    start = pl.multiple_of(pl.program_id(2) * tq, tq)
    lane = lax.broadcasted_iota(I32, (tq, LANES), 1)

    def compute(has_pad_keys):
        if has_pad_keys:
            real = lax.broadcasted_iota(I32, (tq, wk), 1) + start >= pad_rows
        heads = [(pp, h) for pp in range(npg) for h in range(2)]

        def scores(u):
            pp, h = heads[u]
            q2 = q_ref[pp]
            in_head = (lane >= h * dh) & (lane < (h + 1) * dh)
            qh = jnp.where(in_head, q2, jnp.zeros_like(q2))
            kwin = k_ref[pp, pl.ds(start, wk), :]
            return lax.dot_general(qh, kwin, (((1,), (1,)), ((), ())), preferred_element_type=F32)

        def probs(u, s):
            pp, h = heads[u]
            s = s + bias_ref[pp, h]
            if has_pad_keys:
                s = jnp.where(real, s, MASK_VALUE)
            m = jnp.max(s, axis=1, keepdims=True)
            p = jnp.exp2(s - m)
            return p.astype(BF16), jnp.sum(p, axis=1, keepdims=True)

        def weighted(u, p, l):
            pp, _ = heads[u]
            return _dot(p, v_ref[pp, pl.ds(start, wk), :]) / l

        nh = len(heads)
        sc = scores(0)
        pl_prev = None
        outs = {}
        for u in range(nh + 1):
            if u < nh:
                pl_cur = probs(u, sc)
                if u + 1 < nh:
                    sc = scores(u + 1)
            if pl_prev is not None:
                outs[u - 1] = weighted(u - 1, *pl_prev)
                pp, h = heads[u - 1]
                if h == 1:
                    o_ref[pp] = jnp.where(lane < dh, outs[u - 2], outs[u - 1]).astype(o_ref.dtype)
            pl_prev = pl_cur if u < nh else None

    @pl.when(start < pad_rows)
    def _():
        compute(True)

    @pl.when(start >= pad_rows)
    def _():
        compute(False)


def _attn_bias(rel_bias, tq, pad_rows):
    wk = tq + pad_rows
    nh = rel_bias.shape[0]
    n = tq - 1 + wk
    j = jnp.arange(n + 1)
    idx = jnp.clip(tq - 1 + pad_rows - j, -(CHUNK - 1), REL_CLIP) + (CHUNK - 1)
    v = rel_bias[:, idx].astype(F32) * LOG2E
    flat = jnp.tile(v, (1, tq))[:, :tq * n]
    toep = flat.reshape(nh, tq, n)[:, :, tq - 1:tq - 1 + wk]
    qc = jnp.arange(tq)[:, None] // CHUNK
    kc = jnp.arange(wk)[None, :] // CHUNK
    band = (kc >= qc) & (kc <= qc + LEFT_CHUNKS)
    bias = jnp.where(band[None], toep, MASK_VALUE)
    return bias.reshape(nh // 2, 2, tq, wk)


def _chunk_attention(q, kp, vp, bias, pad_rows):
    B, npair, L, _ = q.shape
    tq = bias.shape[2]
    wk = tq + pad_rows
    npg = min(ATTN_PAIRS, npair)
    lp = kp.shape[2]
    return pl.pallas_call(
        functools.partial(_attn_kernel, pad_rows=pad_rows),
        grid=(npair // npg, B, L // tq),
        in_specs=[pl.BlockSpec((None, npg, tq, LANES), lambda g, b, i: (b, g, i, 0)),
                  pl.BlockSpec((None, npg, lp, LANES), lambda g, b, i: (b, g, 0, 0)),
                  pl.BlockSpec((None, npg, lp, LANES), lambda g, b, i: (b, g, 0, 0)),
                  pl.BlockSpec((npg, 2, tq, wk), lambda g, b, i: (g, 0, 0, 0))],
        out_specs=pl.BlockSpec((None, npg, tq, LANES), lambda g, b, i: (b, g, i, 0)),
        out_shape=jax.ShapeDtypeStruct((B, npair, L, LANES), BF16),
        compiler_params=_cparams(("arbitrary", "arbitrary", "arbitrary")),
        name="attn_core",
    )(q, kp, vp, bias)


def _oproj_kernel(a_ref, w_ref, x_ref, g_ref, b_ref, o_ref):
    a = jnp.concatenate([a_ref[p] for p in range(a_ref.shape[0])], axis=1)
    z = DEEPNORM_ALPHA * x_ref[...] + _dot(a, w_ref[...])
    o_ref[...] = _layer_norm(z, g_ref[...], b_ref[...])


def _o_proj(a, w_o, l, x, ln_g, ln_b):
    B, L, D = x.shape
    npair = a.shape[1]
    HD = npair * LANES
    tm = min(ROW_TILE, L)
    return pl.pallas_call(
        _oproj_kernel,
        grid=(B, L // tm),
        in_specs=[pl.BlockSpec((None, npair, tm, LANES), lambda b, i: (b, 0, i, 0)),
                  _layer_spec((HD, D), l),
                  pl.BlockSpec((None, tm, D), lambda b, i: (b, i, 0)),
                  _const_spec((1, D)), _const_spec((1, D))],
        out_specs=pl.BlockSpec((None, tm, D), lambda b, i: (b, i, 0)),
        out_shape=jax.ShapeDtypeStruct((B, L, D), F32),
        compiler_params=_cparams(("arbitrary", "arbitrary")),
        name="attn_o",
    )(a, w_o, x, ln_g, ln_b)


def _attn_layer(x, kp, vp, w_q, rel_bias, w_o, l, ln_g, ln_b, pad_rows):
    B, L, D = x.shape
    HD = w_q.shape[2]
    dh = HD // rel_bias.shape[0]
    tq = min(ATTN_TQ, L)
    q = _q_proj(x, w_q, l, dh ** -0.5 * LOG2E)
    bias = _attn_bias(rel_bias, tq, pad_rows)
    o = _chunk_attention(q, kp, vp, bias, pad_rows)
    return _o_proj(o, w_o, l, x, ln_g, ln_b)


def _pad_ff(w1, w3, w2, tf):
    F = w1.shape[2]
    Fp = -(-F // tf) * tf
    w1 = jnp.pad(w1, ((0, 0), (0, 0), (0, Fp - F))).astype(BF16)
    w3 = jnp.pad(w3, ((0, 0), (0, 0), (0, Fp - F))).astype(BF16)
    w2 = jnp.pad(w2, ((0, 0), (0, Fp - F), (0, 0))).astype(BF16)
    return w1, w3, w2


def kernel(x, ssm_w_in, ssm_log_step, ssm_lam_re, ssm_lam_im, ssm_b_re, ssm_b_im, ssm_c_re, ssm_c_im,
           ssm_d, ssm_w_glu, attn_w_k, attn_w_v, attn_w_q, attn_rel_bias, attn_w_o,
           ffn_w1, ffn_w3, ffn_w2, moe_router, moe_w1, moe_w3, moe_w2, ln_g, ln_b):
    B, L, D = x.shape
    n_a = ssm_w_in.shape[0]
    depth = ln_g.shape[0]
    T = min(S5_T, L)
    pad_rows = LEFT_CHUNKS * CHUNK
    w_in_t = jnp.swapaxes(ssm_w_in, 1, 2).astype(BF16)
    w_glu = ssm_w_glu.astype(BF16)
    w_q, w_o = attn_w_q.astype(BF16), attn_w_o.astype(BF16)
    f1, f3, f2 = _pad_ff(ffn_w1, ffn_w3, ffn_w2, FF_TILE)
    m1, m3, m2 = moe_w1.astype(BF16), moe_w3.astype(BF16), moe_w2.astype(BF16)
    kp = vp = None
    for l in range(depth):
        g0, b0 = ln_g[l, 0].reshape(1, D), ln_b[l, 0].reshape(1, D)
        g1, b1 = ln_g[l, 1].reshape(1, D), ln_b[l, 1].reshape(1, D)
        if l < n_a:
            ops = _s5_params(ssm_log_step[l], ssm_lam_re[l], ssm_lam_im[l], ssm_b_re[l], ssm_b_im[l],
                             ssm_c_re[l], ssm_c_im[l], ssm_d[l], T)
            x = _s5_layer(x, w_in_t, w_glu, l, ops, g0, b0, T)
        else:
            x = _attn_layer(x, kp, vp, w_q, attn_rel_bias[l - n_a], w_o, l - n_a, g0, b0, pad_rows)
        x2 = x.reshape(B * L, D)
        if l % 2 == 0:
            x2 = _ffn_layer(x2, f1, f3, f2, l // 2, g1, b1)
        else:
            x2 = _moe_layer(x2, moe_router[l // 2], m1, m3, m2, l // 2, g1, b1)
        x = x2.reshape(B, L, D)
        if l == n_a - 1:
            kp, vp = _shared_kv(x, attn_w_k.astype(BF16), attn_w_v.astype(BF16), pad_rows)
    return x
```

```python
import functools
import math

import jax
import jax.numpy as jnp
from jax import lax
from jax.experimental import pallas as pl
from jax.experimental.pallas import tpu as pltpu

F32 = jnp.float32
BF16 = jnp.bfloat16
I32 = jnp.int32

DEPTH = 4
CHUNK = 64
SSM_GROUP = 16
M_SHIFT = SSM_GROUP.bit_length() - 1
SSM_STATE = 64
ATTN_HEADS = 32
LEFT_CHUNKS = 8
REL_CLIP = 128
N_EXPERTS = 8
TOP_K = 2
DEEPNORM_ALPHA = (2.0 * DEPTH) ** 0.25
LN_EPS = 1e-5
MASK_VALUE = -1e30
LOG2E = math.log2(math.e)

LANES = 128
VMEM_LIMIT = 56 * 1024 * 1024

S5_T = 64
S5_GROUPS_PER_STEP = 4
ROW_TILE = 512
FFN_TM = 512
FF_TILE = 512
MOE_FF_TILE = 1024
ATTN_TQ = 256
ATTN_PAIRS = 4
ATTN_AHEAD = 1
MOE_TM = 512
ROUTER_TM = 512
DMA_TM = 512
DMA_UNROLL = 16


def _cparams(sem, vmem=VMEM_LIMIT):
    return pltpu.CompilerParams(dimension_semantics=sem, vmem_limit_bytes=vmem)


def _dot(a, b):
    return jnp.dot(a, b, preferred_element_type=F32)


def _split3(a):
    hi = a.astype(BF16)
    r = a - hi.astype(F32)
    mid = r.astype(BF16)
    lo = (r - mid.astype(F32)).astype(BF16)
    return hi, mid, lo


def _dot_f32(a, b):
    ah, am, al = _split3(a)
    bh, bm, bl = _split3(b)
    small = _dot(am, bm) + _dot(ah, bl) + _dot(al, bh)
    return _dot(ah, bh) + (_dot(ah, bm) + _dot(am, bh) + small)


def _dot_3pass(a, b):
    ah = a.astype(BF16)
    al = (a - ah.astype(F32)).astype(BF16)
    bh = b.astype(BF16)
    bl = (b - bh.astype(F32)).astype(BF16)
    return _dot(ah, bh) + (_dot(ah, bl) + _dot(al, bh))


def _copy_dot(a, b, split_left):
    if split_left:
        return sum(_dot(piece, b) for piece in _split3(a))
    return sum(_dot(a, piece) for piece in _split3(b))


def _layer_norm(z, g, b):
    mu = jnp.mean(z, axis=-1, keepdims=True)
    zc = z - mu
    var = jnp.mean(zc * zc, axis=-1, keepdims=True)
    return zc * lax.rsqrt(var + LN_EPS) * g + b


def _sigmoid(x):
    return 1.0 / (1.0 + jnp.exp(-x))


def _gelu_tanh(x):
    c = math.sqrt(2.0 / math.pi)
    return 0.5 * x * (1.0 + jnp.tanh(c * (x + 0.044715 * (x * x * x))))


def _const_spec(shape):
    nd = len(shape)
    return pl.BlockSpec(shape, lambda *_: (0,) * nd, pipeline_mode=pl.Buffered(1))


def _layer_spec(shape, l):
    nd = len(shape)
    return pl.BlockSpec((None,) + tuple(shape), lambda *_: (l,) + (0,) * nd, pipeline_mode=pl.Buffered(1))


def _cpow(er, ei, n, nbits):
    pr = jnp.ones(n.shape, F32)
    pi = jnp.zeros(n.shape, F32)
    br, bi = er, ei
    for k in range(nbits):
        bit = ((n >> k) & 1) == 1
        fr = jnp.where(bit, br, 1.0)
        fi = jnp.where(bit, bi, 0.0)
        pr, pi = pr * fr - pi * fi, pr * fi + pi * fr
        br, bi = br * br - bi * bi, 2.0 * br * bi
    return pr, pi


def _s5_param_kernel(ls_ref, lamc_ref, lamr_ref, btr_ref, bti_ref, ccat_ref, a1_ref, a2_ref, d_ref,
                     at_ref, wt_ref, vt_ref, lt_ref, *, T):
    M = SSM_GROUP
    P = lamc_ref.shape[0]
    TM = T * M
    nb = max(1, int(T).bit_length())
    delta = jnp.exp(ls_ref[...])

    lr = lamc_ref[:, 0:1]
    li = lamc_ref[:, 1:2]
    mag = jnp.exp(lr * delta)
    er = mag * jnp.cos(li * delta)
    ei = mag * jnp.sin(li * delta)
    den = lr * lr + li * li
    nr = er - 1.0
    cfr = (nr * lr + ei * li) / den
    cfi = (ei * lr - nr * li) / den
    jrev = (T - 1) - lax.broadcasted_iota(I32, (P, T), 1)
    pr_t, pi_t = _cpow(er, ei, jrev, nb)
    slot = lax.broadcasted_iota(I32, (T, TM), 1) >> M_SHIFT
    spread = jnp.where(slot == lax.broadcasted_iota(I32, (T, TM), 0), 1.0, 0.0).astype(BF16)
    pr = _copy_dot(pr_t, spread, split_left=True)
    pi = _copy_dot(pi_t, spread, split_left=True)
    btr = btr_ref[...]
    bti = bti_ref[...]
    bbr = cfr * btr - cfi * bti
    bbi = cfr * bti + cfi * btr
    lbr = pr * bbr - pi * bbi
    lbi = pr * bbi + pi * bbr
    wt = jnp.concatenate([lbr, lbi], axis=0)
    wt_ref[...] = wt.astype(BF16)

    krev = _dot_f32(ccat_ref[...], wt)
    row = lax.broadcasted_iota(I32, (M, TM), 0)
    lane2 = lax.broadcasted_iota(I32, (M, TM), 1)
    krev = krev + jnp.where(lane2 == (T - 1) * M + row, d_ref[...], 0.0)
    kext = jnp.concatenate([krev, jnp.zeros_like(krev)], axis=1)
    for t in range(T):
        off = (T - 1 - t) * M
        win = kext if off == 0 else pltpu.roll(kext, 2 * TM - off, axis=1)
        at_ref[t * M:(t + 1) * M, :] = win[:, :TM].astype(BF16)

    lrr = lamr_ref[0:1, :]
    lir = lamr_ref[1:2, :]
    magr = jnp.exp(lrr * delta)
    err = magr * jnp.cos(lir * delta)
    eir = magr * jnp.sin(lir * delta)
    qr_t, qi_t = _cpow(err, eir, lax.broadcasted_iota(I32, (T, 2 * P), 0) + 1, nb)
    rslot = lax.broadcasted_iota(I32, (TM, T), 0) >> M_SHIFT
    rspread = jnp.where(rslot == lax.broadcasted_iota(I32, (TM, T), 1), 1.0, 0.0).astype(BF16)
    qr = _copy_dot(rspread, qr_t, split_left=False)
    qi = _copy_dot(rspread, qi_t, split_left=False)
    vt_ref[...] = (a1_ref[...] * qr + a2_ref[...] * qi).astype(BF16)

    tr, ti = er, ei
    ar = jnp.ones_like(er)
    ai = jnp.zeros_like(ei)
    n = T
    while n:
        if n & 1:
            ar, ai = ar * tr - ai * ti, ar * ti + ai * tr
        tr, ti = tr * tr - ti * ti, 2.0 * tr * ti
        n >>= 1
    lt_ref[...] = jnp.concatenate([ar, ai], axis=0)


def _s5_params(log_step, lam_re, lam_im, b_re, b_im, c_re, c_im, d_skip, T):
    G, P = lam_re.shape
    M = SSM_GROUP
    TM = T * M
    ls = log_step.reshape(G, 1, 1)
    lamc = jnp.stack([lam_re, lam_im], axis=-1)
    lamr = jnp.stack([jnp.tile(lam_re, (1, 2)), jnp.tile(lam_im, (1, 2))], 1)
    btr = jnp.tile(b_re, (1, 1, T))
    bti = jnp.tile(b_im, (1, 1, T))
    ccat = jnp.concatenate([c_re, -c_im], axis=-1)
    a1 = jnp.tile(jnp.concatenate([c_re, -c_im], axis=-1), (1, T, 1))
    a2 = jnp.tile(jnp.concatenate([-c_im, -c_re], axis=-1), (1, T, 1))
    dcol = d_skip.reshape(G, M, 1)

    def gspec(*shape):
        nd = len(shape)
        return pl.BlockSpec((None,) + shape, lambda g: (g,) + (0,) * nd)

    return pl.pallas_call(
        functools.partial(_s5_param_kernel, T=T),
        grid=(G,),
        in_specs=[gspec(1, 1), gspec(P, 2), gspec(2, 2 * P), gspec(P, TM), gspec(P, TM),
                  gspec(M, 2 * P), gspec(TM, 2 * P), gspec(TM, 2 * P), gspec(M, 1)],
        out_specs=[gspec(TM, TM), gspec(2 * P, TM), gspec(TM, 2 * P), gspec(2 * P, 1)],
        out_shape=[jax.ShapeDtypeStruct((G, TM, TM), BF16),
                   jax.ShapeDtypeStruct((G, 2 * P, TM), BF16),
                   jax.ShapeDtypeStruct((G, TM, 2 * P), BF16),
                   jax.ShapeDtypeStruct((G, 2 * P, 1), F32)],
        compiler_params=_cparams(("arbitrary",)),
        name="s5_params",
    )(ls, lamc, lamr, btr, bti, ccat, a1, a2, dcol)


def _s5_uproj_kernel(x_ref, w_ref, o_ref):
    xs = x_ref[...].astype(BF16)
    ut = lax.dot_general(w_ref[...], xs, (((1,), (1,)), ((), ())), preferred_element_type=F32)
    o_ref[...] = ut.astype(BF16).reshape(o_ref.shape)


def _s5_scan_kernel(ut_ref, at_ref, wt_ref, vt_ref, lt_ref, y_ref, *, NC):
    ng, T, M, BC = ut_ref.shape
    P = lt_ref.shape[1] // 2
    cpos = lax.broadcasted_iota(I32, (P, BC), 1) & (NC - 1)

    def matmuls(i):
        u = ut_ref[i].reshape(T * M, BC)
        z = _dot(wt_ref[i], u)
        return z, _dot(at_ref[i], u)

    def entering_state(i, z):
        hr, hi = z[:P], z[P:]
        dr = lt_ref[i, :P, :]
        di = lt_ref[i, P:, :]
        sh = 1
        while sh < NC:
            m = cpos >= sh
            sr = jnp.where(m, pltpu.roll(hr, sh, axis=1), 0.0)
            si = jnp.where(m, pltpu.roll(hi, sh, axis=1), 0.0)
            hr, hi = hr + (dr * sr - di * si), hi + (dr * si + di * sr)
            dr, di = dr * dr - di * di, 2.0 * dr * di
            sh *= 2
        m1 = cpos >= 1
        pr = jnp.where(m1, pltpu.roll(hr, 1, axis=1), 0.0)
        pi = jnp.where(m1, pltpu.roll(hi, 1, axis=1), 0.0)
        return jnp.concatenate([pr, pi], axis=0).astype(BF16)

    nxt = matmuls(0)
    for i in range(ng):
        z, y_local = nxt
        if i + 1 < ng:
            nxt = matmuls(i + 1)
        hp = entering_state(i, z)
        y = y_local + _dot(vt_ref[i], hp)
        y_ref[i] = _gelu_tanh(y).astype(BF16).reshape(T, M, BC)


def _s5_out_kernel(y_ref, w_ref, x_ref, g_ref, b_ref, o_ref):
    G, M, bc = y_ref.shape
    d = x_ref.shape[1]
    yt = y_ref[...].reshape(G * M, bc)
    o = lax.dot_general(yt, w_ref[...], (((0,), (0,)), ((), ())), preferred_element_type=F32)
    mix = o[:, :d] * _sigmoid(o[:, d:])
    z = DEEPNORM_ALPHA * x_ref[...] + mix
    o_ref[...] = _layer_norm(z, g_ref[...], b_ref[...])


def _s5_layer(x, w_in_t, w_glu, l, ops, ln_g, ln_b, T):
    B, L, D = x.shape
    H = w_in_t.shape[1]
    M = SSM_GROUP
    G = H // M
    NC = L // T
    BC = B * NC
    at, wt, vt, lt = ops
    P2 = wt.shape[1]
    x3 = x.reshape(BC, T * D)

    ut = pl.pallas_call(
        _s5_uproj_kernel,
        grid=(T,),
        in_specs=[pl.BlockSpec((BC, D), lambda s: (0, s)), _layer_spec((H, D), l)],
        out_specs=pl.BlockSpec((G, None, M, BC), lambda s: (0, s, 0, 0)),
        out_shape=jax.ShapeDtypeStruct((G, T, M, BC), BF16),
        compiler_params=_cparams(("arbitrary",)),
        name="s5_uproj",
    )(x3, w_in_t)

    ng = S5_GROUPS_PER_STEP if G % S5_GROUPS_PER_STEP == 0 else 1
    ys = pl.pallas_call(
        functools.partial(_s5_scan_kernel, NC=NC),
        grid=(G // ng,),
        in_specs=[pl.BlockSpec((ng, T, M, BC), lambda g: (g, 0, 0, 0)),
                  pl.BlockSpec((ng, T * M, T * M), lambda g: (g, 0, 0)),
                  pl.BlockSpec((ng, P2, T * M), lambda g: (g, 0, 0)),
                  pl.BlockSpec((ng, T * M, P2), lambda g: (g, 0, 0)),
                  pl.BlockSpec((ng, P2, 1), lambda g: (g, 0, 0))],
        out_specs=pl.BlockSpec((ng, T, M, BC), lambda g: (g, 0, 0, 0)),
        out_shape=jax.ShapeDtypeStruct((G, T, M, BC), BF16),
        compiler_params=_cparams(("arbitrary",)),
        name="s5_scan",
    )(ut, at, wt, vt, lt)

    nsplit = 2 if (BC // 2) % LANES == 0 else 1
    bct = BC // nsplit
    out = pl.pallas_call(
        _s5_out_kernel,
        grid=(T, nsplit),
        in_specs=[pl.BlockSpec((G, None, M, bct), lambda s, h: (0, s, 0, h)),
                  _layer_spec((H, 2 * D), l),
                  pl.BlockSpec((bct, D), lambda s, h: (h, s)),
                  _const_spec((1, D)), _const_spec((1, D))],
        out_specs=pl.BlockSpec((bct, D), lambda s, h: (h, s)),
        out_shape=jax.ShapeDtypeStruct((BC, T * D), F32),
        compiler_params=_cparams(("arbitrary", "arbitrary")),
        name="s5_out",
    )(ys, w_glu, x3, ln_g, ln_b)
    return out.reshape(B, L, D)


def _swiglu_step(xb_ref, acc_ref, w1_ref, w3_ref, w2_ref):
    xb = xb_ref[...]
    h1 = _dot(xb, w1_ref[...])
    h3 = _dot(xb, w3_ref[...])
    h = (h1 * _sigmoid(h1) * h3).astype(BF16)
    acc_ref[...] += _dot(h, w2_ref[...])


def _ffn_kernel(x_ref, w1_ref, w3_ref, w2_ref, g_ref, b_ref, o_ref, xb_ref):
    j = pl.program_id(1)

    @pl.when(j == 0)
    def _():
        xb_ref[...] = x_ref[...].astype(BF16)
        o_ref[...] = jnp.zeros_like(o_ref)

    _swiglu_step(xb_ref, o_ref, w1_ref, w3_ref, w2_ref)

    @pl.when(j == pl.num_programs(1) - 1)
    def _():
        z = DEEPNORM_ALPHA * x_ref[...] + o_ref[...]
        o_ref[...] = _layer_norm(z, g_ref[...], b_ref[...])


def _ffn_layer(x2, w1, w3, w2, l, ln_g, ln_b, tm=None, tf=FF_TILE):
    N, D = x2.shape
    Fp = w1.shape[2]
    tm = min(FFN_TM if tm is None else tm, N)
    tf = min(tf, Fp)
    return pl.pallas_call(
        _ffn_kernel,
        grid=(N // tm, Fp // tf),
        in_specs=[pl.BlockSpec((tm, D), lambda i, j: (i, 0)),
                  pl.BlockSpec((None, D, tf), lambda i, j: (l, 0, j)),
                  pl.BlockSpec((None, D, tf), lambda i, j: (l, 0, j)),
                  pl.BlockSpec((None, tf, D), lambda i, j: (l, j, 0)),
                  _const_spec((1, D)), _const_spec((1, D))],
        out_specs=pl.BlockSpec((tm, D), lambda i, j: (i, 0)),
        out_shape=jax.ShapeDtypeStruct((N, D), F32),
        scratch_shapes=[pltpu.VMEM((tm, D), BF16)],
        compiler_params=_cparams(("arbitrary", "arbitrary")),
        name="ffn_dense",
    )(x2, w1, w3, w2, ln_g, ln_b)


def _expert_kernel(be_ref, nb_ref, x_ref, w1_ref, w3_ref, w2_ref, y_ref, xb_ref):
    i = pl.program_id(0)
    j = pl.program_id(1)

    @pl.when(j == 0)
    def _():
        y_ref[...] = jnp.zeros_like(y_ref)

    @pl.when(i < nb_ref[0])
    def _():
        @pl.when(j == 0)
        def _():
            xb_ref[...] = x_ref[...].astype(BF16)

        _swiglu_step(xb_ref, y_ref, w1_ref, w3_ref, w2_ref)


def _expert_ffn(xs, block_exp, nb_used, w1, w3, w2, l, tm, tf=None):
    Pm, D = xs.shape
    _, E, _, F = w1.shape
    tf = MOE_FF_TILE if tf is None else tf
    tf = min(tf, F)
    nblk = Pm // tm
    nj = F // tf

    def row_map(i, j, be, nb):
        return (jnp.minimum(i, nb[0] - 1), 0)

    def jj(i, j, nb):
        return jnp.where(i < nb[0], j, nj - 1)

    def w13_map(i, j, be, nb):
        return (l, be[jnp.minimum(i, nb[0] - 1)], 0, jj(i, j, nb))

    def w2_map(i, j, be, nb):
        return (l, be[jnp.minimum(i, nb[0] - 1)], jj(i, j, nb), 0)

    grid_spec = pltpu.PrefetchScalarGridSpec(
        num_scalar_prefetch=2,
        grid=(nblk, nj),
        in_specs=[pl.BlockSpec((tm, D), row_map),
                  pl.BlockSpec((None, None, D, tf), w13_map),
                  pl.BlockSpec((None, None, D, tf), w13_map),
                  pl.BlockSpec((None, None, tf, D), w2_map)],
        out_specs=pl.BlockSpec((tm, D), lambda i, j, be, nb: (i, 0)),
        scratch_shapes=[pltpu.VMEM((tm, D), BF16)],
    )
    return pl.pallas_call(
        _expert_kernel,
        grid_spec=grid_spec,
        out_shape=jax.ShapeDtypeStruct((Pm, D), F32),
        compiler_params=_cparams(("arbitrary", "arbitrary")),
        name="moe_experts",
    )(block_exp, nb_used, xs, w1, w3, w2)


def _router_kernel(x_ref, wr_ref, e_ref, gate_ref, rank_ref, cnt_ref, carry_ref, *, E):
    i = pl.program_id(0)

    @pl.when(i == 0)
    def _():
        carry_ref[...] = jnp.zeros_like(carry_ref)

    tm = x_ref.shape[0]
    logits = _dot_3pass(x_ref[...], wr_ref[...])
    lane = lax.broadcasted_iota(I32, logits.shape, 1).astype(F32)
    neg = jnp.float32(-3.0e38)
    lg = jnp.where(lane < E, logits, neg)
    m1 = jnp.max(lg, axis=1, keepdims=True)
    i1 = jnp.min(jnp.where(lg == m1, lane, float(LANES)), axis=1, keepdims=True)
    lg2 = jnp.where(lane == i1, neg, lg)
    m2 = jnp.max(lg2, axis=1, keepdims=True)
    i2 = jnp.min(jnp.where(lg2 == m2, lane, float(LANES)), axis=1, keepdims=True)
    t = jnp.exp(m2 - m1)
    g1 = 1.0 / (1.0 + t)
    g2 = t * g1

    oh1 = lane == i1
    oh2 = lane == i2
    sel = jnp.where(oh1 | oh2, 1.0, 0.0)
    rr = lax.broadcasted_iota(I32, (tm, tm), 0)
    cc = lax.broadcasted_iota(I32, (tm, tm), 1)
    lower = jnp.where(rr > cc, 1.0, 0.0).astype(BF16)
    before = _dot(lower, sel.astype(BF16)) + carry_ref[...]
    r1 = jnp.sum(jnp.where(oh1, before, 0.0), axis=1, keepdims=True)
    r2 = jnp.sum(jnp.where(oh2, before, 0.0), axis=1, keepdims=True)
    carry_ref[...] += jnp.sum(sel, axis=0, keepdims=True)

    e_ref[:, 0:1] = i1.astype(I32)
    e_ref[:, 1:2] = i2.astype(I32)
    gate_ref[:, 0:1] = g1
    gate_ref[:, 1:2] = g2
    rank_ref[:, 0:1] = r1.astype(I32)
    rank_ref[:, 1:2] = r2.astype(I32)
    cnt_ref[...] = carry_ref[...].astype(I32)


def _router(x2, w_router):
    N, D = x2.shape
    E = w_router.shape[1]
    tm = min(ROUTER_TM, N)
    wr = jnp.pad(w_router, ((0, 0), (0, LANES - E)))
    return pl.pallas_call(
        functools.partial(_router_kernel, E=E),
        grid=(N // tm,),
        in_specs=[pl.BlockSpec((tm, D), lambda i: (i, 0)), _const_spec((D, LANES))],
        out_specs=[pl.BlockSpec((tm, TOP_K), lambda i: (i, 0)),
                   pl.BlockSpec((tm, TOP_K), lambda i: (i, 0)),
                   pl.BlockSpec((tm, TOP_K), lambda i: (i, 0)),
                   pl.BlockSpec((1, LANES), lambda i: (0, 0))],
        out_shape=[jax.ShapeDtypeStruct((N, TOP_K), I32),
                   jax.ShapeDtypeStruct((N, TOP_K), F32),
                   jax.ShapeDtypeStruct((N, TOP_K), I32),
                   jax.ShapeDtypeStruct((1, LANES), I32)],
        scratch_shapes=[pltpu.VMEM((1, LANES), F32)],
        compiler_params=_cparams(("arbitrary",)),
        name="moe_router",
    )(x2, wr)


def _row_copy(src, src_row, dst, dst_row, sem):
    return pltpu.make_async_copy(src.at[pl.ds(src_row, 1), :], dst.at[pl.ds(dst_row, 1), :], sem)


def _dispatch_kernel(pend_ref, nbu_ref, dest_ref, x_ref, xs_hbm, zero_ref, sem, zsem, *, tm, block_rows, n_blocks):
    parts = block_rows // tm

    def zero_block(row0):
        row0 = pl.multiple_of(row0, tm)
        return [pltpu.make_async_copy(zero_ref, xs_hbm.at[pl.ds(row0 + k * tm, tm), :], zsem) for k in range(parts)]

    @pl.when(pl.program_id(0) == 0)
    def _():
        zero_ref[...] = jnp.zeros_like(zero_ref)

        def fill(row0):
            copies = zero_block(row0)
            for cp in copies:
                cp.start()
            for cp in copies:
                cp.wait()

        for e in range(pend_ref.shape[0]):
            fill(jnp.maximum(pend_ref[e] - block_rows, 0))

        def unused(j, c):
            fill(j * block_rows)
            return c

        lax.fori_loop(nbu_ref[0], n_blocks, unused, 0)

    def issue(r, c):
        for k in range(TOP_K):
            _row_copy(x_ref, r, xs_hbm, dest_ref[TOP_K * r + k], sem).start()
        return c

    lax.fori_loop(0, tm, issue, 0, unroll=DMA_UNROLL)

    def drain(r, c):
        for k in range(TOP_K):
            _row_copy(x_ref, r, xs_hbm, dest_ref[TOP_K * r + k], sem).wait()
        return c

    lax.fori_loop(0, tm, drain, 0, unroll=DMA_UNROLL)


def _dispatch(x2, dest_flat, pend, nb_used, block_rows, n_blocks):
    N, D = x2.shape
    tm = min(DMA_TM, N, block_rows)
    grid_spec = pltpu.PrefetchScalarGridSpec(
        num_scalar_prefetch=2,
        grid=(N // tm,),
        in_specs=[pl.BlockSpec((TOP_K * tm,), lambda i, pe, nb: (i,), memory_space=pltpu.SMEM),
                  pl.BlockSpec((tm, D), lambda i, pe, nb: (i, 0))],
        out_specs=pl.BlockSpec(memory_space=pl.ANY),
        scratch_shapes=[pltpu.VMEM((tm, D), F32), pltpu.SemaphoreType.DMA(()), pltpu.SemaphoreType.DMA(())],
    )
    return pl.pallas_call(
        functools.partial(_dispatch_kernel, tm=tm, block_rows=block_rows, n_blocks=n_blocks),
        grid_spec=grid_spec,
        out_shape=jax.ShapeDtypeStruct((n_blocks * block_rows, D), F32),
        compiler_params=_cparams(("arbitrary",)),
        name="moe_dispatch",
    )(pend, nb_used, dest_flat, x2)


def _combine_kernel(dest_ref, ys_hbm, x_ref, gate_ref, g_ref, b_ref, o_ref, buf_ref, sem, *, tm):
    def issue(r, c):
        for k in range(TOP_K):
            _row_copy(ys_hbm, dest_ref[TOP_K * r + k], buf_ref.at[k], r, sem).start()
        return c

    lax.fori_loop(0, tm, issue, 0, unroll=DMA_UNROLL)

    def drain(r, c):
        for k in range(TOP_K):
            _row_copy(ys_hbm, dest_ref[TOP_K * r + k], buf_ref.at[k], r, sem).wait()
        return c

    lax.fori_loop(0, tm, drain, 0, unroll=DMA_UNROLL)
    f = gate_ref[:, 0:1] * buf_ref[0] + gate_ref[:, 1:2] * buf_ref[1]
    z = DEEPNORM_ALPHA * x_ref[...] + f
    o_ref[...] = _layer_norm(z, g_ref[...], b_ref[...])


def _combine(ys, dest_flat, x2, gates, ln_g, ln_b):
    N, D = x2.shape
    tm = min(DMA_TM, N)
    return pl.pallas_call(
        functools.partial(_combine_kernel, tm=tm),
        grid=(N // tm,),
        in_specs=[pl.BlockSpec((TOP_K * tm,), lambda i: (i,), memory_space=pltpu.SMEM),
                  pl.BlockSpec(memory_space=pl.ANY),
                  pl.BlockSpec((tm, D), lambda i: (i, 0)),
                  pl.BlockSpec((tm, TOP_K), lambda i: (i, 0)),
                  _const_spec((1, D)), _const_spec((1, D))],
        out_specs=pl.BlockSpec((tm, D), lambda i: (i, 0)),
        out_shape=jax.ShapeDtypeStruct((N, D), F32),
        scratch_shapes=[pltpu.VMEM((TOP_K, tm, D), F32), pltpu.SemaphoreType.DMA(())],
        compiler_params=_cparams(("arbitrary",)),
        name="moe_combine",
    )(dest_flat, ys, x2, gates, ln_g, ln_b)


def _moe_layer(x2, w_router, w1, w3, w2, l, ln_g, ln_b, tm=MOE_TM):
    N, D = x2.shape
    E = w_router.shape[1]
    eidx, gates, rank, cnt = _router(x2, w_router)
    counts = cnt[0, :E]
    padded = ((counts + tm - 1) // tm) * tm
    pend = jnp.cumsum(padded)
    pstart = pend - padded
    n_assign = N * TOP_K
    nblk = -(-n_assign // tm) + E
    dest = (pstart[eidx] + rank).reshape(-1).astype(I32)
    block_exp = jnp.minimum(
        jnp.searchsorted(pend, jnp.arange(nblk, dtype=I32) * tm, side="right"), E - 1).astype(I32)
    nb_used = (pend[-1] // tm).astype(I32).reshape(1)
    xs = _dispatch(x2, dest, pend.astype(I32), nb_used, tm, nblk)
    ys = _expert_ffn(xs, block_exp, nb_used, w1, w3, w2, l, tm)
    return _combine(ys, dest, x2, gates, ln_g, ln_b)


def _store_head_pairs(o_ref, r):
    for p in range(o_ref.shape[0]):
        o_ref[p] = r[:, p * LANES:(p + 1) * LANES].astype(o_ref.dtype)


def _kv_kernel(x_ref, wk_ref, wv_ref, k_ref, v_ref, *, pad_blocks):
    i = pl.program_id(1)

    @pl.when(i < pad_blocks)
    def _():
        k_ref[...] = jnp.zeros_like(k_ref)
        v_ref[...] = jnp.zeros_like(v_ref)

    @pl.when(i >= pad_blocks)
    def _():
        xb = x_ref[...].astype(BF16)
        _store_head_pairs(k_ref, _dot(xb, wk_ref[...]))
        _store_head_pairs(v_ref, _dot(xb, wv_ref[...]))


def _shared_kv(x, w_k, w_v, pad_rows):
    B, L, D = x.shape
    HD = w_k.shape[1]
    npair = HD // LANES
    tm = min(ROW_TILE, pad_rows)
    pad_blocks = pad_rows // tm
    spec_o = pl.BlockSpec((None, npair, tm, LANES), lambda b, i: (b, 0, i, 0))
    return pl.pallas_call(
        functools.partial(_kv_kernel, pad_blocks=pad_blocks),
        grid=(B, (L + pad_rows) // tm),
        in_specs=[pl.BlockSpec((None, tm, D), lambda b, i: (b, jnp.maximum(i - pad_blocks, 0), 0)),
                  _const_spec((D, HD)), _const_spec((D, HD))],
        out_specs=[spec_o, spec_o],
        out_shape=[jax.ShapeDtypeStruct((B, npair, L + pad_rows, LANES), BF16)] * 2,
        compiler_params=_cparams(("arbitrary", "arbitrary")),
        name="attn_kv",
    )(x, w_k, w_v)


def _qproj_kernel(x_ref, w_ref, o_ref, *, scale):
    _store_head_pairs(o_ref, _dot(x_ref[...].astype(BF16), w_ref[...]) * scale)


def _q_proj(x, w_q, l, scale):
    B, L, D = x.shape
    HD = w_q.shape[2]
    npair = HD // LANES
    tm = min(ROW_TILE, L)
    return pl.pallas_call(
        functools.partial(_qproj_kernel, scale=scale),
        grid=(B, L // tm),
        in_specs=[pl.BlockSpec((None, tm, D), lambda b, i: (b, i, 0)), _layer_spec((D, HD), l)],
        out_specs=pl.BlockSpec((None, npair, tm, LANES), lambda b, i: (b, 0, i, 0)),
        out_shape=jax.ShapeDtypeStruct((B, npair, L, LANES), BF16),
        compiler_params=_cparams(("arbitrary", "arbitrary")),
        name="attn_q",
    )(x, w_q)


def _attn_kernel(q_ref, k_ref, v_ref, bias_ref, o_ref, *, pad_rows):
    npg, tq, _ = q_ref.shape
    wk = tq + pad_rows
    dh = LANES // 2
    start = pl.multiple_of(pl.program_id(2) * tq, tq)
    lane = lax.broadcasted_iota(I32, (tq, LANES), 1)

    def compute(has_pad_keys):
        if has_pad_keys:
            real = lax.broadcasted_iota(I32, (tq, wk), 1) + start >= pad_rows
        heads = [(pp, h) for pp in range(npg) for h in range(2)]

        def scores(u):
            pp, h = heads[u]
            q2 = q_ref[pp]
            in_head = (lane >= h * dh) & (lane < (h + 1) * dh)
            qh = jnp.where(in_head, q2, jnp.zeros_like(q2))
            kwin = k_ref[pp, pl.ds(start, wk), :]
            return lax.dot_general(qh, kwin, (((1,), (1,)), ((), ())), preferred_element_type=F32)

        def probs(u, s):
            pp, h = heads[u]
            s = s + bias_ref[pp, h]
            if has_pad_keys:
                s = jnp.where(real, s, MASK_VALUE)
            m = jnp.max(s, axis=1, keepdims=True)
            p = jnp.exp2(s - m)
            return p.astype(BF16), jnp.sum(p, axis=1, keepdims=True)

        def weighted(u, p, l):
            pp, _ = heads[u]
            return _dot(p, v_ref[pp, pl.ds(start, wk), :]) / l

        nh = len(heads)
        sc = {u: scores(u) for u in range(min(ATTN_AHEAD, nh))}
        pl_prev = None
        outs = {}
        for u in range(nh + 1):
            if u < nh:
                pl_cur = probs(u, sc.pop(u))
                if u + ATTN_AHEAD < nh:
                    sc[u + ATTN_AHEAD] = scores(u + ATTN_AHEAD)
            if pl_prev is not None:
                outs[u - 1] = weighted(u - 1, *pl_prev)
                pp, h = heads[u - 1]
                if h == 1:
                    o_ref[pp] = jnp.where(lane < dh, outs[u - 2], outs[u - 1]).astype(o_ref.dtype)
            pl_prev = pl_cur if u < nh else None

    @pl.when(start < pad_rows)
    def _():
        compute(True)

    @pl.when(start >= pad_rows)
    def _():
        compute(False)


def _attn_bias(rel_bias, tq, pad_rows):
    wk = tq + pad_rows
    nh = rel_bias.shape[0]
    n = tq - 1 + wk
    j = jnp.arange(n + 1)
    idx = jnp.clip(tq - 1 + pad_rows - j, -(CHUNK - 1), REL_CLIP) + (CHUNK - 1)
    v = rel_bias[:, idx].astype(F32) * LOG2E
    flat = jnp.tile(v, (1, tq))[:, :tq * n]
    toep = flat.reshape(nh, tq, n)[:, :, tq - 1:tq - 1 + wk]
    qc = jnp.arange(tq)[:, None] // CHUNK
    kc = jnp.arange(wk)[None, :] // CHUNK
    band = (kc >= qc) & (kc <= qc + LEFT_CHUNKS)
    bias = jnp.where(band[None], toep, MASK_VALUE)
    return bias.reshape(nh // 2, 2, tq, wk)


def _chunk_attention(q, kp, vp, bias, pad_rows):
    B, npair, L, _ = q.shape
    tq = bias.shape[2]
    wk = tq + pad_rows
    npg = min(ATTN_PAIRS, npair)
    lp = kp.shape[2]
    return pl.pallas_call(
        functools.partial(_attn_kernel, pad_rows=pad_rows),
        grid=(npair // npg, B, L // tq),
        in_specs=[pl.BlockSpec((None, npg, tq, LANES), lambda g, b, i: (b, g, i, 0)),
                  pl.BlockSpec((None, npg, lp, LANES), lambda g, b, i: (b, g, 0, 0)),
                  pl.BlockSpec((None, npg, lp, LANES), lambda g, b, i: (b, g, 0, 0)),
                  pl.BlockSpec((npg, 2, tq, wk), lambda g, b, i: (g, 0, 0, 0))],
        out_specs=pl.BlockSpec((None, npg, tq, LANES), lambda g, b, i: (b, g, i, 0)),
        out_shape=jax.ShapeDtypeStruct((B, npair, L, LANES), BF16),
        compiler_params=_cparams(("arbitrary", "arbitrary", "arbitrary")),
        name="attn_core",
    )(q, kp, vp, bias)


def _oproj_kernel(a_ref, w_ref, x_ref, g_ref, b_ref, o_ref):
    a = jnp.concatenate([a_ref[p] for p in range(a_ref.shape[0])], axis=1)
    z = DEEPNORM_ALPHA * x_ref[...] + _dot(a, w_ref[...])
    o_ref[...] = _layer_norm(z, g_ref[...], b_ref[...])


def _o_proj(a, w_o, l, x, ln_g, ln_b):
    B, L, D = x.shape
    npair = a.shape[1]
    HD = npair * LANES
    tm = min(ROW_TILE, L)
    return pl.pallas_call(
        _oproj_kernel,
        grid=(B, L // tm),
        in_specs=[pl.BlockSpec((None, npair, tm, LANES), lambda b, i: (b, 0, i, 0)),
                  _layer_spec((HD, D), l),
                  pl.BlockSpec((None, tm, D), lambda b, i: (b, i, 0)),
                  _const_spec((1, D)), _const_spec((1, D))],
        out_specs=pl.BlockSpec((None, tm, D), lambda b, i: (b, i, 0)),
        out_shape=jax.ShapeDtypeStruct((B, L, D), F32),
        compiler_params=_cparams(("arbitrary", "arbitrary")),
        name="attn_o",
    )(a, w_o, x, ln_g, ln_b)


def _attn_layer(x, kp, vp, w_q, rel_bias, w_o, l, ln_g, ln_b, pad_rows):
    B, L, D = x.shape
    HD = w_q.shape[2]
    dh = HD // rel_bias.shape[0]
    tq = min(ATTN_TQ, L)
    q = _q_proj(x, w_q, l, dh ** -0.5 * LOG2E)
    bias = _attn_bias(rel_bias, tq, pad_rows)
    o = _chunk_attention(q, kp, vp, bias, pad_rows)
    return _o_proj(o, w_o, l, x, ln_g, ln_b)


def _pad_ff(w1, w3, w2, tf):
    F = w1.shape[2]
    Fp = -(-F // tf) * tf
    w1 = jnp.pad(w1, ((0, 0), (0, 0), (0, Fp - F))).astype(BF16)
    w3 = jnp.pad(w3, ((0, 0), (0, 0), (0, Fp - F))).astype(BF16)
    w2 = jnp.pad(w2, ((0, 0), (0, Fp - F), (0, 0))).astype(BF16)
    return w1, w3, w2


def kernel(x, ssm_w_in, ssm_log_step, ssm_lam_re, ssm_lam_im, ssm_b_re, ssm_b_im, ssm_c_re, ssm_c_im,
           ssm_d, ssm_w_glu, attn_w_k, attn_w_v, attn_w_q, attn_rel_bias, attn_w_o,
           ffn_w1, ffn_w3, ffn_w2, moe_router, moe_w1, moe_w3, moe_w2, ln_g, ln_b):
    B, L, D = x.shape
    n_a = ssm_w_in.shape[0]
    depth = ln_g.shape[0]
    T = min(S5_T, L)
    pad_rows = LEFT_CHUNKS * CHUNK
    w_in_t = jnp.swapaxes(ssm_w_in, 1, 2).astype(BF16)
    w_glu = ssm_w_glu.astype(BF16)
    w_q, w_o = attn_w_q.astype(BF16), attn_w_o.astype(BF16)
    f1, f3, f2 = _pad_ff(ffn_w1, ffn_w3, ffn_w2, FF_TILE)
    m1, m3, m2 = moe_w1.astype(BF16), moe_w3.astype(BF16), moe_w2.astype(BF16)
    kp = vp = None
    for l in range(depth):
        g0, b0 = ln_g[l, 0].reshape(1, D), ln_b[l, 0].reshape(1, D)
        g1, b1 = ln_g[l, 1].reshape(1, D), ln_b[l, 1].reshape(1, D)
        if l < n_a:
            ops = _s5_params(ssm_log_step[l], ssm_lam_re[l], ssm_lam_im[l], ssm_b_re[l], ssm_b_im[l],
                             ssm_c_re[l], ssm_c_im[l], ssm_d[l], T)
            x = _s5_layer(x, w_in_t, w_glu, l, ops, g0, b0, T)
        else:
            x = _attn_layer(x, kp, vp, w_q, attn_rel_bias[l - n_a], w_o, l - n_a, g0, b0, pad_rows)
        x2 = x.reshape(B * L, D)
        if l % 2 == 0:
            x2 = _ffn_layer(x2, f1, f3, f2, l // 2, g1, b1)
        else:
            x2 = _moe_layer(x2, moe_router[l // 2], m1, m3, m2, l // 2, g1, b1)
        x = x2.reshape(B, L, D)
        if l == n_a - 1:
            kp, vp = _shared_kv(x, attn_w_k.astype(BF16), attn_w_v.astype(BF16), pad_rows)
    return x
```

```python
import functools
import math

import jax
import jax.numpy as jnp
from jax import lax
from jax.experimental import pallas as pl
from jax.experimental.pallas import tpu as pltpu

F32 = jnp.float32
BF16 = jnp.bfloat16
I32 = jnp.int32

DEPTH = 4
CHUNK = 64
SSM_GROUP = 16
M_SHIFT = SSM_GROUP.bit_length() - 1
SSM_STATE = 64
ATTN_HEADS = 32
LEFT_CHUNKS = 8
REL_CLIP = 128
N_EXPERTS = 8
TOP_K = 2
DEEPNORM_ALPHA = (2.0 * DEPTH) ** 0.25
LN_EPS = 1e-5
MASK_VALUE = -1e30
LOG2E = math.log2(math.e)

LANES = 128
VMEM_LIMIT = 56 * 1024 * 1024

S5_T = 64
S5_GROUPS_PER_STEP = 4
ROW_TILE = 512
FFN_TM = 512
FF_TILE = 512
MOE_FF_TILE = 1024
ATTN_TQ = 256
ATTN_PAIRS = 4
ATTN_AHEAD = 1
MOE_TM = 512
ROUTER_TM = 512
DMA_TM = 512
DMA_UNROLL = 16


def _cparams(sem, vmem=VMEM_LIMIT):
    return pltpu.CompilerParams(dimension_semantics=sem, vmem_limit_bytes=vmem)


def _dot(a, b):
    return jnp.dot(a, b, preferred_element_type=F32)


def _split3(a):
    hi = a.astype(BF16)
    r = a - hi.astype(F32)
    mid = r.astype(BF16)
    lo = (r - mid.astype(F32)).astype(BF16)
    return hi, mid, lo


def _dot_f32(a, b):
    ah, am, al = _split3(a)
    bh, bm, bl = _split3(b)
    small = _dot(am, bm) + _dot(ah, bl) + _dot(al, bh)
    return _dot(ah, bh) + (_dot(ah, bm) + _dot(am, bh) + small)


def _dot_3pass(a, b):
    ah = a.astype(BF16)
    al = (a - ah.astype(F32)).astype(BF16)
    bh = b.astype(BF16)
    bl = (b - bh.astype(F32)).astype(BF16)
    return _dot(ah, bh) + (_dot(ah, bl) + _dot(al, bh))


def _copy_dot(a, b, split_left):
    if split_left:
        return sum(_dot(piece, b) for piece in _split3(a))
    return sum(_dot(a, piece) for piece in _split3(b))


def _layer_norm(z, g, b):
    mu = jnp.mean(z, axis=-1, keepdims=True)
    zc = z - mu
    var = jnp.mean(zc * zc, axis=-1, keepdims=True)
    return zc * lax.rsqrt(var + LN_EPS) * g + b


def _sigmoid(x):
    return 1.0 / (1.0 + jnp.exp(-x))


def _gelu_tanh(x):
    c = math.sqrt(2.0 / math.pi)
    return 0.5 * x * (1.0 + jnp.tanh(c * (x + 0.044715 * (x * x * x))))


def _const_spec(shape):
    nd = len(shape)
    return pl.BlockSpec(shape, lambda *_: (0,) * nd, pipeline_mode=pl.Buffered(1))


def _layer_spec(shape, l):
    nd = len(shape)
    return pl.BlockSpec((None,) + tuple(shape), lambda *_: (l,) + (0,) * nd, pipeline_mode=pl.Buffered(1))


def _cpow(er, ei, n, nbits):
    pr = jnp.ones(n.shape, F32)
    pi = jnp.zeros(n.shape, F32)
    br, bi = er, ei
    for k in range(nbits):
        bit = ((n >> k) & 1) == 1
        fr = jnp.where(bit, br, 1.0)
        fi = jnp.where(bit, bi, 0.0)
        pr, pi = pr * fr - pi * fi, pr * fi + pi * fr
        br, bi = br * br - bi * bi, 2.0 * br * bi
    return pr, pi


def _s5_param_kernel(ls_ref, lamc_ref, lamr_ref, btr_ref, bti_ref, ccat_ref, a1_ref, a2_ref, d_ref,
                     at_ref, wt_ref, vt_ref, lt_ref, *, T):
    M = SSM_GROUP
    P = lamc_ref.shape[0]
    TM = T * M
    nb = max(1, int(T).bit_length())
    delta = jnp.exp(ls_ref[...])

    lr = lamc_ref[:, 0:1]
    li = lamc_ref[:, 1:2]
    mag = jnp.exp(lr * delta)
    er = mag * jnp.cos(li * delta)
    ei = mag * jnp.sin(li * delta)
    den = lr * lr + li * li
    nr = er - 1.0
    cfr = (nr * lr + ei * li) / den
    cfi = (ei * lr - nr * li) / den
    jrev = (T - 1) - lax.broadcasted_iota(I32, (P, T), 1)
    pr_t, pi_t = _cpow(er, ei, jrev, nb)
    slot = lax.broadcasted_iota(I32, (T, TM), 1) >> M_SHIFT
    spread = jnp.where(slot == lax.broadcasted_iota(I32, (T, TM), 0), 1.0, 0.0).astype(BF16)
    pr = _copy_dot(pr_t, spread, split_left=True)
    pi = _copy_dot(pi_t, spread, split_left=True)
    btr = btr_ref[...]
    bti = bti_ref[...]
    bbr = cfr * btr - cfi * bti
    bbi = cfr * bti + cfi * btr
    lbr = pr * bbr - pi * bbi
    lbi = pr * bbi + pi * bbr
    wt = jnp.concatenate([lbr, lbi], axis=0)
    wt_ref[...] = wt.astype(BF16)

    krev = _dot_f32(ccat_ref[...], wt)
    row = lax.broadcasted_iota(I32, (M, TM), 0)
    lane2 = lax.broadcasted_iota(I32, (M, TM), 1)
    krev = krev + jnp.where(lane2 == (T - 1) * M + row, d_ref[...], 0.0)
    kext = jnp.concatenate([krev, jnp.zeros_like(krev)], axis=1)
    for t in range(T):
        off = (T - 1 - t) * M
        win = kext if off == 0 else pltpu.roll(kext, 2 * TM - off, axis=1)
        at_ref[t * M:(t + 1) * M, :] = win[:, :TM].astype(BF16)

    lrr = lamr_ref[0:1, :]
    lir = lamr_ref[1:2, :]
    magr = jnp.exp(lrr * delta)
    err = magr * jnp.cos(lir * delta)
    eir = magr * jnp.sin(lir * delta)
    qr_t, qi_t = _cpow(err, eir, lax.broadcasted_iota(I32, (T, 2 * P), 0) + 1, nb)
    rslot = lax.broadcasted_iota(I32, (TM, T), 0) >> M_SHIFT
    rspread = jnp.where(rslot == lax.broadcasted_iota(I32, (TM, T), 1), 1.0, 0.0).astype(BF16)
    qr = _copy_dot(rspread, qr_t, split_left=False)
    qi = _copy_dot(rspread, qi_t, split_left=False)
    vt_ref[...] = (a1_ref[...] * qr + a2_ref[...] * qi).astype(BF16)

    tr, ti = er, ei
    ar = jnp.ones_like(er)
    ai = jnp.zeros_like(ei)
    n = T
    while n:
        if n & 1:
            ar, ai = ar * tr - ai * ti, ar * ti + ai * tr
        tr, ti = tr * tr - ti * ti, 2.0 * tr * ti
        n >>= 1
    lt_ref[...] = jnp.concatenate([ar, ai], axis=0)


def _s5_params(log_step, lam_re, lam_im, b_re, b_im, c_re, c_im, d_skip, T):
    G, P = lam_re.shape
    M = SSM_GROUP
    TM = T * M
    ls = log_step.reshape(G, 1, 1)
    lamc = jnp.stack([lam_re, lam_im], axis=-1)
    lamr = jnp.stack([jnp.tile(lam_re, (1, 2)), jnp.tile(lam_im, (1, 2))], 1)
    btr = jnp.tile(b_re, (1, 1, T))
    bti = jnp.tile(b_im, (1, 1, T))
    ccat = jnp.concatenate([c_re, -c_im], axis=-1)
    a1 = jnp.tile(jnp.concatenate([c_re, -c_im], axis=-1), (1, T, 1))
    a2 = jnp.tile(jnp.concatenate([-c_im, -c_re], axis=-1), (1, T, 1))
    dcol = d_skip.reshape(G, M, 1)

    def gspec(*shape):
        nd = len(shape)
        return pl.BlockSpec((None,) + shape, lambda g: (g,) + (0,) * nd)

    return pl.pallas_call(
        functools.partial(_s5_param_kernel, T=T),
        grid=(G,),
        in_specs=[gspec(1, 1), gspec(P, 2), gspec(2, 2 * P), gspec(P, TM), gspec(P, TM),
                  gspec(M, 2 * P), gspec(TM, 2 * P), gspec(TM, 2 * P), gspec(M, 1)],
        out_specs=[gspec(TM, TM), gspec(2 * P, TM), gspec(TM, 2 * P), gspec(2 * P, 1)],
        out_shape=[jax.ShapeDtypeStruct((G, TM, TM), BF16),
                   jax.ShapeDtypeStruct((G, 2 * P, TM), BF16),
                   jax.ShapeDtypeStruct((G, TM, 2 * P), BF16),
                   jax.ShapeDtypeStruct((G, 2 * P, 1), F32)],
        compiler_params=_cparams(("arbitrary",)),
        name="s5_params",
    )(ls, lamc, lamr, btr, bti, ccat, a1, a2, dcol)


def _s5_uproj_kernel(x_ref, w_ref, o_ref):
    xs = x_ref[...].astype(BF16)
    ut = lax.dot_general(w_ref[...], xs, (((1,), (1,)), ((), ())), preferred_element_type=F32)
    o_ref[...] = ut.astype(BF16).reshape(o_ref.shape)


def _s5_scan_kernel(ut_ref, at_ref, wt_ref, vt_ref, lt_ref, y_ref, *, NC):
    ng, T, M, BC = ut_ref.shape
    P = lt_ref.shape[1] // 2
    cpos = lax.broadcasted_iota(I32, (P, BC), 1) & (NC - 1)

    def matmuls(i):
        u = ut_ref[i].reshape(T * M, BC)
        z = _dot(wt_ref[i], u)
        return z, _dot(at_ref[i], u)

    def entering_state(i, z):
        hr, hi = z[:P], z[P:]
        dr = lt_ref[i, :P, :]
        di = lt_ref[i, P:, :]
        sh = 1
        while sh < NC:
            m = cpos >= sh
            sr = jnp.where(m, pltpu.roll(hr, sh, axis=1), 0.0)
            si = jnp.where(m, pltpu.roll(hi, sh, axis=1), 0.0)
            hr, hi = hr + (dr * sr - di * si), hi + (dr * si + di * sr)
            dr, di = dr * dr - di * di, 2.0 * dr * di
            sh *= 2
        m1 = cpos >= 1
        pr = jnp.where(m1, pltpu.roll(hr, 1, axis=1), 0.0)
        pi = jnp.where(m1, pltpu.roll(hi, 1, axis=1), 0.0)
        return jnp.concatenate([pr, pi], axis=0).astype(BF16)

    nxt = matmuls(0)
    for i in range(ng):
        z, y_local = nxt
        if i + 1 < ng:
            nxt = matmuls(i + 1)
        hp = entering_state(i, z)
        y = y_local + _dot(vt_ref[i], hp)
        y_ref[i] = _gelu_tanh(y).astype(BF16).reshape(T, M, BC)


def _s5_out_kernel(y_ref, w_ref, x_ref, g_ref, b_ref, o_ref):
    G, M, bc = y_ref.shape
    d = x_ref.shape[1]
    yt = y_ref[...].reshape(G * M, bc)
    o = lax.dot_general(yt, w_ref[...], (((0,), (0,)), ((), ())), preferred_element_type=F32)
    mix = o[:, :d] * _sigmoid(o[:, d:])
    z = DEEPNORM_ALPHA * x_ref[...] + mix
    o_ref[...] = _layer_norm(z, g_ref[...], b_ref[...])


def _s5_layer(x3, w_in_t, w_glu, l, ops, ln_g, ln_b, T, NC):
    BC, TD = x3.shape
    D = TD // T
    H = w_in_t.shape[1]
    M = SSM_GROUP
    G = H // M
    at, wt, vt, lt = ops
    P2 = wt.shape[1]

    ut = pl.pallas_call(
        _s5_uproj_kernel,
        grid=(T,),
        in_specs=[pl.BlockSpec((BC, D), lambda s: (0, s)), _layer_spec((H, D), l)],
        out_specs=pl.BlockSpec((G, None, M, BC), lambda s: (0, s, 0, 0)),
        out_shape=jax.ShapeDtypeStruct((G, T, M, BC), BF16),
        compiler_params=_cparams(("arbitrary",)),
        name="s5_uproj",
    )(x3, w_in_t)

    ng = S5_GROUPS_PER_STEP if G % S5_GROUPS_PER_STEP == 0 else 1
    ys = pl.pallas_call(
        functools.partial(_s5_scan_kernel, NC=NC),
        grid=(G // ng,),
        in_specs=[pl.BlockSpec((ng, T, M, BC), lambda g: (g, 0, 0, 0)),
                  pl.BlockSpec((ng, T * M, T * M), lambda g: (g, 0, 0)),
                  pl.BlockSpec((ng, P2, T * M), lambda g: (g, 0, 0)),
                  pl.BlockSpec((ng, T * M, P2), lambda g: (g, 0, 0)),
                  pl.BlockSpec((ng, P2, 1), lambda g: (g, 0, 0))],
        out_specs=pl.BlockSpec((ng, T, M, BC), lambda g: (g, 0, 0, 0)),
        out_shape=jax.ShapeDtypeStruct((G, T, M, BC), BF16),
        compiler_params=_cparams(("arbitrary",)),
        name="s5_scan",
    )(ut, at, wt, vt, lt)

    nsplit = 2 if (BC // 2) % LANES == 0 else 1
    bct = BC // nsplit
    out = pl.pallas_call(
        _s5_out_kernel,
        grid=(T, nsplit),
        in_specs=[pl.BlockSpec((G, None, M, bct), lambda s, h: (0, s, 0, h)),
                  _layer_spec((H, 2 * D), l),
                  pl.BlockSpec((bct, D), lambda s, h: (h, s)),
                  _const_spec((1, D)), _const_spec((1, D))],
        out_specs=pl.BlockSpec((bct, D), lambda s, h: (h, s)),
        out_shape=jax.ShapeDtypeStruct((BC, T * D), F32),
        compiler_params=_cparams(("arbitrary", "arbitrary")),
        name="s5_out",
    )(ys, w_glu, x3, ln_g, ln_b)
    return out


def _swiglu_step(xb_ref, acc_ref, w1_ref, w3_ref, w2_ref):
    xb = xb_ref[...]
    h1 = _dot(xb, w1_ref[...])
    h3 = _dot(xb, w3_ref[...])
    h = (h1 * _sigmoid(h1) * h3).astype(BF16)
    acc_ref[...] += _dot(h, w2_ref[...])


def _ffn_kernel(x_ref, w1_ref, w3_ref, w2_ref, g_ref, b_ref, o_ref, xb_ref):
    j = pl.program_id(1)

    @pl.when(j == 0)
    def _():
        xb_ref[...] = x_ref[...].astype(BF16)
        o_ref[...] = jnp.zeros_like(o_ref)

    _swiglu_step(xb_ref, o_ref, w1_ref, w3_ref, w2_ref)

    @pl.when(j == pl.num_programs(1) - 1)
    def _():
        z = DEEPNORM_ALPHA * x_ref[...] + o_ref[...]
        o_ref[...] = _layer_norm(z, g_ref[...], b_ref[...])


def _ffn_gathered_kernel(x_ref, w1_ref, w3_ref, w2_ref, g_ref, b_ref, o_ref, xb_ref, acc_ref, *, T):
    j = pl.program_id(1)
    d = x_ref.shape[1] // T
    rows = x_ref.shape[0]

    def tokens():
        return jnp.concatenate([x_ref[:, s * d:(s + 1) * d] for s in range(T)], axis=0)

    @pl.when(j == 0)
    def _():
        xb_ref[...] = tokens().astype(BF16)
        acc_ref[...] = jnp.zeros_like(acc_ref)

    _swiglu_step(xb_ref, acc_ref, w1_ref, w3_ref, w2_ref)

    @pl.when(j == pl.num_programs(1) - 1)
    def _():
        y = _layer_norm(DEEPNORM_ALPHA * tokens() + acc_ref[...], g_ref[...], b_ref[...])
        for s in range(T):
            o_ref[:, s * d:(s + 1) * d] = y[s * rows:(s + 1) * rows, :]


def _ffn_layer_gathered(xg, w1, w3, w2, l, ln_g, ln_b, T, tf=FF_TILE):
    BC, TD = xg.shape
    D = TD // T
    Fp = w1.shape[2]
    rows = 8
    tf = min(tf, Fp)
    return pl.pallas_call(
        functools.partial(_ffn_gathered_kernel, T=T),
        grid=(BC // rows, Fp // tf),
        in_specs=[pl.BlockSpec((rows, TD), lambda i, j: (i, 0)),
                  pl.BlockSpec((None, D, tf), lambda i, j: (l, 0, j)),
                  pl.BlockSpec((None, D, tf), lambda i, j: (l, 0, j)),
                  pl.BlockSpec((None, tf, D), lambda i, j: (l, j, 0)),
                  _const_spec((1, D)), _const_spec((1, D))],
        out_specs=pl.BlockSpec((rows, TD), lambda i, j: (i, 0)),
        out_shape=jax.ShapeDtypeStruct((BC, TD), F32),
        scratch_shapes=[pltpu.VMEM((rows * T, D), BF16), pltpu.VMEM((rows * T, D), F32)],
        compiler_params=_cparams(("arbitrary", "arbitrary")),
        name="ffn_dense_s5",
    )(xg, w1, w3, w2, ln_g, ln_b)


def _ffn_layer(x2, w1, w3, w2, l, ln_g, ln_b, tm=None, tf=FF_TILE):
    N, D = x2.shape
    Fp = w1.shape[2]
    tm = min(FFN_TM if tm is None else tm, N)
    tf = min(tf, Fp)
    return pl.pallas_call(
        _ffn_kernel,
        grid=(N // tm, Fp // tf),
        in_specs=[pl.BlockSpec((tm, D), lambda i, j: (i, 0)),
                  pl.BlockSpec((None, D, tf), lambda i, j: (l, 0, j)),
                  pl.BlockSpec((None, D, tf), lambda i, j: (l, 0, j)),
                  pl.BlockSpec((None, tf, D), lambda i, j: (l, j, 0)),
                  _const_spec((1, D)), _const_spec((1, D))],
        out_specs=pl.BlockSpec((tm, D), lambda i, j: (i, 0)),
        out_shape=jax.ShapeDtypeStruct((N, D), F32),
        scratch_shapes=[pltpu.VMEM((tm, D), BF16)],
        compiler_params=_cparams(("arbitrary", "arbitrary")),
        name="ffn_dense",
    )(x2, w1, w3, w2, ln_g, ln_b)


def _expert_kernel(be_ref, nb_ref, x_ref, w1_ref, w3_ref, w2_ref, y_ref, xb_ref):
    i = pl.program_id(0)
    j = pl.program_id(1)

    @pl.when(j == 0)
    def _():
        y_ref[...] = jnp.zeros_like(y_ref)

    @pl.when(i < nb_ref[0])
    def _():
        @pl.when(j == 0)
        def _():
            xb_ref[...] = x_ref[...].astype(BF16)

        _swiglu_step(xb_ref, y_ref, w1_ref, w3_ref, w2_ref)


def _expert_ffn(xs, block_exp, nb_used, w1, w3, w2, l, tm, tf=None):
    Pm, D = xs.shape
    _, E, _, F = w1.shape
    tf = MOE_FF_TILE if tf is None else tf
    tf = min(tf, F)
    nblk = Pm // tm
    nj = F // tf

    def row_map(i, j, be, nb):
        return (jnp.minimum(i, nb[0] - 1), 0)

    def jj(i, j, nb):
        return jnp.where(i < nb[0], j, nj - 1)

    def w13_map(i, j, be, nb):
        return (l, be[jnp.minimum(i, nb[0] - 1)], 0, jj(i, j, nb))

    def w2_map(i, j, be, nb):
        return (l, be[jnp.minimum(i, nb[0] - 1)], jj(i, j, nb), 0)

    grid_spec = pltpu.PrefetchScalarGridSpec(
        num_scalar_prefetch=2,
        grid=(nblk, nj),
        in_specs=[pl.BlockSpec((tm, D), row_map),
                  pl.BlockSpec((None, None, D, tf), w13_map),
                  pl.BlockSpec((None, None, D, tf), w13_map),
                  pl.BlockSpec((None, None, tf, D), w2_map)],
        out_specs=pl.BlockSpec((tm, D), lambda i, j, be, nb: (i, 0)),
        scratch_shapes=[pltpu.VMEM((tm, D), BF16)],
    )
    return pl.pallas_call(
        _expert_kernel,
        grid_spec=grid_spec,
        out_shape=jax.ShapeDtypeStruct((Pm, D), F32),
        compiler_params=_cparams(("arbitrary", "arbitrary")),
        name="moe_experts",
    )(block_exp, nb_used, xs, w1, w3, w2)


def _router_kernel(x_ref, wr_ref, e_ref, gate_ref, rank_ref, cnt_ref, carry_ref, *, E):
    i = pl.program_id(0)

    @pl.when(i == 0)
    def _():
        carry_ref[...] = jnp.zeros_like(carry_ref)

    tm = x_ref.shape[0]
    logits = _dot_3pass(x_ref[...], wr_ref[...])
    lane = lax.broadcasted_iota(I32, logits.shape, 1).astype(F32)
    neg = jnp.float32(-3.0e38)
    lg = jnp.where(lane < E, logits, neg)
    m1 = jnp.max(lg, axis=1, keepdims=True)
    i1 = jnp.min(jnp.where(lg == m1, lane, float(LANES)), axis=1, keepdims=True)
    lg2 = jnp.where(lane == i1, neg, lg)
    m2 = jnp.max(lg2, axis=1, keepdims=True)
    i2 = jnp.min(jnp.where(lg2 == m2, lane, float(LANES)), axis=1, keepdims=True)
    t = jnp.exp(m2 - m1)
    g1 = 1.0 / (1.0 + t)
    g2 = t * g1

    oh1 = lane == i1
    oh2 = lane == i2
    sel = jnp.where(oh1 | oh2, 1.0, 0.0)
    rr = lax.broadcasted_iota(I32, (tm, tm), 0)
    cc = lax.broadcasted_iota(I32, (tm, tm), 1)
    lower = jnp.where(rr > cc, 1.0, 0.0).astype(BF16)
    before = _dot(lower, sel.astype(BF16)) + carry_ref[...]
    r1 = jnp.sum(jnp.where(oh1, before, 0.0), axis=1, keepdims=True)
    r2 = jnp.sum(jnp.where(oh2, before, 0.0), axis=1, keepdims=True)
    carry_ref[...] += jnp.sum(sel, axis=0, keepdims=True)

    e_ref[:, 0:1] = i1.astype(I32)
    e_ref[:, 1:2] = i2.astype(I32)
    gate_ref[:, 0:1] = g1
    gate_ref[:, 1:2] = g2
    rank_ref[:, 0:1] = r1.astype(I32)
    rank_ref[:, 1:2] = r2.astype(I32)
    cnt_ref[...] = carry_ref[...].astype(I32)


def _router(x2, w_router):
    N, D = x2.shape
    E = w_router.shape[1]
    tm = min(ROUTER_TM, N)
    wr = jnp.pad(w_router, ((0, 0), (0, LANES - E)))
    return pl.pallas_call(
        functools.partial(_router_kernel, E=E),
        grid=(N // tm,),
        in_specs=[pl.BlockSpec((tm, D), lambda i: (i, 0)), _const_spec((D, LANES))],
        out_specs=[pl.BlockSpec((tm, TOP_K), lambda i: (i, 0)),
                   pl.BlockSpec((tm, TOP_K), lambda i: (i, 0)),
                   pl.BlockSpec((tm, TOP_K), lambda i: (i, 0)),
                   pl.BlockSpec((1, LANES), lambda i: (0, 0))],
        out_shape=[jax.ShapeDtypeStruct((N, TOP_K), I32),
                   jax.ShapeDtypeStruct((N, TOP_K), F32),
                   jax.ShapeDtypeStruct((N, TOP_K), I32),
                   jax.ShapeDtypeStruct((1, LANES), I32)],
        scratch_shapes=[pltpu.VMEM((1, LANES), F32)],
        compiler_params=_cparams(("arbitrary",)),
        name="moe_router",
    )(x2, wr)


def _row_copy(src, src_row, dst, dst_row, sem):
    return pltpu.make_async_copy(src.at[pl.ds(src_row, 1), :], dst.at[pl.ds(dst_row, 1), :], sem)


def _dispatch_kernel(pend_ref, nbu_ref, dest_ref, x_ref, xs_hbm, zero_ref, sem, zsem, *, tm, block_rows, n_blocks):
    parts = block_rows // tm

    def zero_block(row0):
        row0 = pl.multiple_of(row0, tm)
        return [pltpu.make_async_copy(zero_ref, xs_hbm.at[pl.ds(row0 + k * tm, tm), :], zsem) for k in range(parts)]

    @pl.when(pl.program_id(0) == 0)
    def _():
        zero_ref[...] = jnp.zeros_like(zero_ref)

        def fill(row0):
            copies = zero_block(row0)
            for cp in copies:
                cp.start()
            for cp in copies:
                cp.wait()

        for e in range(pend_ref.shape[0]):
            fill(jnp.maximum(pend_ref[e] - block_rows, 0))

        def unused(j, c):
            fill(j * block_rows)
            return c

        lax.fori_loop(nbu_ref[0], n_blocks, unused, 0)

    def issue(r, c):
        for k in range(TOP_K):
            _row_copy(x_ref, r, xs_hbm, dest_ref[TOP_K * r + k], sem).start()
        return c

    lax.fori_loop(0, tm, issue, 0, unroll=DMA_UNROLL)

    def drain(r, c):
        for k in range(TOP_K):
            _row_copy(x_ref, r, xs_hbm, dest_ref[TOP_K * r + k], sem).wait()
        return c

    lax.fori_loop(0, tm, drain, 0, unroll=DMA_UNROLL)


def _dispatch(x2, dest_flat, pend, nb_used, block_rows, n_blocks):
    N, D = x2.shape
    tm = min(DMA_TM, N, block_rows)
    grid_spec = pltpu.PrefetchScalarGridSpec(
        num_scalar_prefetch=2,
        grid=(N // tm,),
        in_specs=[pl.BlockSpec((TOP_K * tm,), lambda i, pe, nb: (i,), memory_space=pltpu.SMEM),
                  pl.BlockSpec((tm, D), lambda i, pe, nb: (i, 0))],
        out_specs=pl.BlockSpec(memory_space=pl.ANY),
        scratch_shapes=[pltpu.VMEM((tm, D), F32), pltpu.SemaphoreType.DMA(()), pltpu.SemaphoreType.DMA(())],
    )
    return pl.pallas_call(
        functools.partial(_dispatch_kernel, tm=tm, block_rows=block_rows, n_blocks=n_blocks),
        grid_spec=grid_spec,
        out_shape=jax.ShapeDtypeStruct((n_blocks * block_rows, D), F32),
        compiler_params=_cparams(("arbitrary",)),
        name="moe_dispatch",
    )(pend, nb_used, dest_flat, x2)


def _combine_kernel(dest_ref, ys_hbm, x_ref, gate_ref, g_ref, b_ref, o_ref, buf_ref, sem, *, tm):
    def issue(r, c):
        for k in range(TOP_K):
            _row_copy(ys_hbm, dest_ref[TOP_K * r + k], buf_ref.at[k], r, sem).start()
        return c

    lax.fori_loop(0, tm, issue, 0, unroll=DMA_UNROLL)

    def drain(r, c):
        for k in range(TOP_K):
            _row_copy(ys_hbm, dest_ref[TOP_K * r + k], buf_ref.at[k], r, sem).wait()
        return c

    lax.fori_loop(0, tm, drain, 0, unroll=DMA_UNROLL)
    f = gate_ref[:, 0:1] * buf_ref[0] + gate_ref[:, 1:2] * buf_ref[1]
    z = DEEPNORM_ALPHA * x_ref[...] + f
    o_ref[...] = _layer_norm(z, g_ref[...], b_ref[...])


def _combine(ys, dest_flat, x2, gates, ln_g, ln_b):
    N, D = x2.shape
    tm = min(DMA_TM, N)
    return pl.pallas_call(
        functools.partial(_combine_kernel, tm=tm),
        grid=(N // tm,),
        in_specs=[pl.BlockSpec((TOP_K * tm,), lambda i: (i,), memory_space=pltpu.SMEM),
                  pl.BlockSpec(memory_space=pl.ANY),
                  pl.BlockSpec((tm, D), lambda i: (i, 0)),
                  pl.BlockSpec((tm, TOP_K), lambda i: (i, 0)),
                  _const_spec((1, D)), _const_spec((1, D))],
        out_specs=pl.BlockSpec((tm, D), lambda i: (i, 0)),
        out_shape=jax.ShapeDtypeStruct((N, D), F32),
        scratch_shapes=[pltpu.VMEM((TOP_K, tm, D), F32), pltpu.SemaphoreType.DMA(())],
        compiler_params=_cparams(("arbitrary",)),
        name="moe_combine",
    )(dest_flat, ys, x2, gates, ln_g, ln_b)


def _moe_layer(x2, w_router, w1, w3, w2, l, ln_g, ln_b, tm=MOE_TM):
    N, D = x2.shape
    E = w_router.shape[1]
    eidx, gates, rank, cnt = _router(x2, w_router)
    counts = cnt[0, :E]
    padded = ((counts + tm - 1) // tm) * tm
    pend = jnp.cumsum(padded)
    pstart = pend - padded
    n_assign = N * TOP_K
    nblk = -(-n_assign // tm) + E
    dest = (pstart[eidx] + rank).reshape(-1).astype(I32)
    block_exp = jnp.minimum(
        jnp.searchsorted(pend, jnp.arange(nblk, dtype=I32) * tm, side="right"), E - 1).astype(I32)
    nb_used = (pend[-1] // tm).astype(I32).reshape(1)
    xs = _dispatch(x2, dest, pend.astype(I32), nb_used, tm, nblk)
    ys = _expert_ffn(xs, block_exp, nb_used, w1, w3, w2, l, tm)
    return _combine(ys, dest, x2, gates, ln_g, ln_b)


def _store_head_pairs(o_ref, r):
    for p in range(o_ref.shape[0]):
        o_ref[p] = r[:, p * LANES:(p + 1) * LANES].astype(o_ref.dtype)


def _kv_kernel(x_ref, wk_ref, wv_ref, k_ref, v_ref, *, pad_blocks):
    i = pl.program_id(1)

    @pl.when(i < pad_blocks)
    def _():
        k_ref[...] = jnp.zeros_like(k_ref)
        v_ref[...] = jnp.zeros_like(v_ref)

    @pl.when(i >= pad_blocks)
    def _():
        xb = x_ref[...].astype(BF16)
        _store_head_pairs(k_ref, _dot(xb, wk_ref[...]))
        _store_head_pairs(v_ref, _dot(xb, wv_ref[...]))


def _shared_kv(x, w_k, w_v, pad_rows):
    B, L, D = x.shape
    HD = w_k.shape[1]
    npair = HD // LANES
    tm = min(ROW_TILE, pad_rows)
    pad_blocks = pad_rows // tm
    spec_o = pl.BlockSpec((None, npair, tm, LANES), lambda b, i: (b, 0, i, 0))
    return pl.pallas_call(
        functools.partial(_kv_kernel, pad_blocks=pad_blocks),
        grid=(B, (L + pad_rows) // tm),
        in_specs=[pl.BlockSpec((None, tm, D), lambda b, i: (b, jnp.maximum(i - pad_blocks, 0), 0)),
                  _const_spec((D, HD)), _const_spec((D, HD))],
        out_specs=[spec_o, spec_o],
        out_shape=[jax.ShapeDtypeStruct((B, npair, L + pad_rows, LANES), BF16)] * 2,
        compiler_params=_cparams(("arbitrary", "arbitrary")),
        name="attn_kv",
    )(x, w_k, w_v)


def _qproj_kernel(x_ref, w_ref, o_ref, *, scale):
    _store_head_pairs(o_ref, _dot(x_ref[...].astype(BF16), w_ref[...]) * scale)


def _q_proj(x, w_q, l, scale):
    B, L, D = x.shape
    HD = w_q.shape[2]
    npair = HD // LANES
    tm = min(ROW_TILE, L)
    return pl.pallas_call(
        functools.partial(_qproj_kernel, scale=scale),
        grid=(B, L // tm),
        in_specs=[pl.BlockSpec((None, tm, D), lambda b, i: (b, i, 0)), _layer_spec((D, HD), l)],
        out_specs=pl.BlockSpec((None, npair, tm, LANES), lambda b, i: (b, 0, i, 0)),
        out_shape=jax.ShapeDtypeStruct((B, npair, L, LANES), BF16),
        compiler_params=_cparams(("arbitrary", "arbitrary")),
        name="attn_q",
    )(x, w_q)


def _attn_kernel(q_ref, k_ref, v_ref, bias_ref, o_ref, *, pad_rows):
    npg, tq, _ = q_ref.shape
    wk = tq + pad_rows
    dh = LANES // 2
    start = pl.multiple_of(pl.program_id(2) * tq, tq)
    lane = lax.broadcasted_iota(I32, (tq, LANES), 1)

    def compute(has_pad_keys):
        if has_pad_keys:
            real = lax.broadcasted_iota(I32, (tq, wk), 1) + start >= pad_rows
        heads = [(pp, h) for pp in range(npg) for h in range(2)]

        def scores(u):
            pp, h = heads[u]
            q2 = q_ref[pp]
            in_head = (lane >= h * dh) & (lane < (h + 1) * dh)
            qh = jnp.where(in_head, q2, jnp.zeros_like(q2))
            kwin = k_ref[pp, pl.ds(start, wk), :]
            return lax.dot_general(qh, kwin, (((1,), (1,)), ((), ())), preferred_element_type=F32)

        def probs(u, s):
            pp, h = heads[u]
            s = s + bias_ref[pp, h]
            if has_pad_keys:
                s = jnp.where(real, s, MASK_VALUE)
            m = jnp.max(s, axis=1, keepdims=True)
            p = jnp.exp2(s - m)
            return p.astype(BF16), jnp.sum(p, axis=1, keepdims=True)

        def weighted(u, p, l):
            pp, _ = heads[u]
            return _dot(p, v_ref[pp, pl.ds(start, wk), :]) / l

        nh = len(heads)
        sc = {u: scores(u) for u in range(min(ATTN_AHEAD, nh))}
        pl_prev = None
        outs = {}
        for u in range(nh + 1):
            if u < nh:
                pl_cur = probs(u, sc.pop(u))
                if u + ATTN_AHEAD < nh:
                    sc[u + ATTN_AHEAD] = scores(u + ATTN_AHEAD)
            if pl_prev is not None:
                outs[u - 1] = weighted(u - 1, *pl_prev)
                pp, h = heads[u - 1]
                if h == 1:
                    o_ref[pp] = jnp.where(lane < dh, outs[u - 2], outs[u - 1]).astype(o_ref.dtype)
            pl_prev = pl_cur if u < nh else None

    @pl.when(start < pad_rows)
    def _():
        compute(True)

    @pl.when(start >= pad_rows)
    def _():
        compute(False)


def _attn_bias(rel_bias, tq, pad_rows):
    wk = tq + pad_rows
    nh = rel_bias.shape[0]
    n = tq - 1 + wk
    j = jnp.arange(n + 1)
    idx = jnp.clip(tq - 1 + pad_rows - j, -(CHUNK - 1), REL_CLIP) + (CHUNK - 1)
    v = rel_bias[:, idx].astype(F32) * LOG2E
    flat = jnp.tile(v, (1, tq))[:, :tq * n]
    toep = flat.reshape(nh, tq, n)[:, :, tq - 1:tq - 1 + wk]
    qc = jnp.arange(tq)[:, None] // CHUNK
    kc = jnp.arange(wk)[None, :] // CHUNK
    band = (kc >= qc) & (kc <= qc + LEFT_CHUNKS)
    bias = jnp.where(band[None], toep, MASK_VALUE)
    return bias.reshape(nh // 2, 2, tq, wk)


def _chunk_attention(q, kp, vp, bias, pad_rows):
    B, npair, L, _ = q.shape
    tq = bias.shape[2]
    wk = tq + pad_rows
    npg = min(ATTN_PAIRS, npair)
    lp = kp.shape[2]
    return pl.pallas_call(
        functools.partial(_attn_kernel, pad_rows=pad_rows),
        grid=(npair // npg, B, L // tq),
        in_specs=[pl.BlockSpec((None, npg, tq, LANES), lambda g, b, i: (b, g, i, 0)),
                  pl.BlockSpec((None, npg, lp, LANES), lambda g, b, i: (b, g, 0, 0)),
                  pl.BlockSpec((None, npg, lp, LANES), lambda g, b, i: (b, g, 0, 0)),
                  pl.BlockSpec((npg, 2, tq, wk), lambda g, b, i: (g, 0, 0, 0))],
        out_specs=pl.BlockSpec((None, npg, tq, LANES), lambda g, b, i: (b, g, i, 0)),
        out_shape=jax.ShapeDtypeStruct((B, npair, L, LANES), BF16),
        compiler_params=_cparams(("arbitrary", "arbitrary", "arbitrary")),
        name="attn_core",
    )(q, kp, vp, bias)


def _oproj_kernel(a_ref, w_ref, x_ref, g_ref, b_ref, o_ref):
    a = jnp.concatenate([a_ref[p] for p in range(a_ref.shape[0])], axis=1)
    z = DEEPNORM_ALPHA * x_ref[...] + _dot(a, w_ref[...])
    o_ref[...] = _layer_norm(z, g_ref[...], b_ref[...])


def _o_proj(a, w_o, l, x, ln_g, ln_b):
    B, L, D = x.shape
    npair = a.shape[1]
    HD = npair * LANES
    tm = min(ROW_TILE, L)
    return pl.pallas_call(
        _oproj_kernel,
        grid=(B, L // tm),
        in_specs=[pl.BlockSpec((None, npair, tm, LANES), lambda b, i: (b, 0, i, 0)),
                  _layer_spec((HD, D), l),
                  pl.BlockSpec((None, tm, D), lambda b, i: (b, i, 0)),
                  _const_spec((1, D)), _const_spec((1, D))],
        out_specs=pl.BlockSpec((None, tm, D), lambda b, i: (b, i, 0)),
        out_shape=jax.ShapeDtypeStruct((B, L, D), F32),
        compiler_params=_cparams(("arbitrary", "arbitrary")),
        name="attn_o",
    )(a, w_o, x, ln_g, ln_b)


def _attn_layer(x, kp, vp, w_q, rel_bias, w_o, l, ln_g, ln_b, pad_rows):
    B, L, D = x.shape
    HD = w_q.shape[2]
    dh = HD // rel_bias.shape[0]
    tq = min(ATTN_TQ, L)
    q = _q_proj(x, w_q, l, dh ** -0.5 * LOG2E)
    bias = _attn_bias(rel_bias, tq, pad_rows)
    o = _chunk_attention(q, kp, vp, bias, pad_rows)
    return _o_proj(o, w_o, l, x, ln_g, ln_b)


def _pad_ff(w1, w3, w2, tf):
    F = w1.shape[2]
    Fp = -(-F // tf) * tf
    w1 = jnp.pad(w1, ((0, 0), (0, 0), (0, Fp - F))).astype(BF16)
    w3 = jnp.pad(w3, ((0, 0), (0, 0), (0, Fp - F))).astype(BF16)
    w2 = jnp.pad(w2, ((0, 0), (0, Fp - F), (0, 0))).astype(BF16)
    return w1, w3, w2


def kernel(x, ssm_w_in, ssm_log_step, ssm_lam_re, ssm_lam_im, ssm_b_re, ssm_b_im, ssm_c_re, ssm_c_im,
           ssm_d, ssm_w_glu, attn_w_k, attn_w_v, attn_w_q, attn_rel_bias, attn_w_o,
           ffn_w1, ffn_w3, ffn_w2, moe_router, moe_w1, moe_w3, moe_w2, ln_g, ln_b):
    B, L, D = x.shape
    n_a = ssm_w_in.shape[0]
    depth = ln_g.shape[0]
    T = min(S5_T, L)
    pad_rows = LEFT_CHUNKS * CHUNK
    w_in_t = jnp.swapaxes(ssm_w_in, 1, 2).astype(BF16)
    w_glu = ssm_w_glu.astype(BF16)
    w_q, w_o = attn_w_q.astype(BF16), attn_w_o.astype(BF16)
    f1, f3, f2 = _pad_ff(ffn_w1, ffn_w3, ffn_w2, FF_TILE)
    m1, m3, m2 = moe_w1.astype(BF16), moe_w3.astype(BF16), moe_w2.astype(BF16)
    kp = vp = None
    xg_next = None
    for l in range(depth):
        g0, b0 = ln_g[l, 0].reshape(1, D), ln_b[l, 0].reshape(1, D)
        g1, b1 = ln_g[l, 1].reshape(1, D), ln_b[l, 1].reshape(1, D)
        if l < n_a:
            ops = _s5_params(ssm_log_step[l], ssm_lam_re[l], ssm_lam_im[l], ssm_b_re[l], ssm_b_im[l],
                             ssm_c_re[l], ssm_c_im[l], ssm_d[l], T)
            xg = xg_next if xg_next is not None else x.reshape(B * (L // T), T * D)
            xg = _s5_layer(xg, w_in_t, w_glu, l, ops, g0, b0, T, L // T)
            xg_next = None
            if l % 2 == 0 and l + 1 < n_a:
                xg_next = _ffn_layer_gathered(xg, f1, f3, f2, l // 2, g1, b1, T)
                continue
            x = xg.reshape(B, L, D)
        else:
            x = _attn_layer(x, kp, vp, w_q, attn_rel_bias[l - n_a], w_o, l - n_a, g0, b0, pad_rows)
        x2 = x.reshape(B * L, D)
        if l % 2 == 0:
            x2 = _ffn_layer(x2, f1, f3, f2, l // 2, g1, b1)
        else:
            x2 = _moe_layer(x2, moe_router[l // 2], m1, m3, m2, l // 2, g1, b1)
        x = x2.reshape(B, L, D)
        if l == n_a - 1:
            kp, vp = _shared_kv(x, attn_w_k.astype(BF16), attn_w_v.astype(BF16), pad_rows)
    return x
```
